```python
import math
import jax, jax.numpy as jnp
from jax import lax
import numpy as np

D_MODEL = 1024
BATCH = 16
SEQ = 4096
DEPTH = 4

CHUNK = 64
N_PREV_CHUNKS = 8
BAND = CHUNK * (N_PREV_CHUNKS + 1)
MAX_REL = 128
HEAD_DIM = 64
MIX_WIDTH = D_MODEL
D_SB = MIX_WIDTH // 2
D_CA = MIX_WIDTH - D_SB
H_SB = D_SB // HEAD_DIM
H_CA = D_CA // HEAD_DIM
SB_QBLOCK = 128
D_FF = int(round(8 * D_MODEL / 3 / 256)) * 256
FFN_RESIDUAL = 0.5
ALPHA = (2 * DEPTH) ** 0.25
BETA_INIT = (8 * DEPTH) ** -0.25
LN_EPS = 1e-5
RMS_EPS = 1e-6

kernel_name = "hybrid_stickbreak_chunkattn_macaron_deepnorm"


def layer_norm(x, g, b):
    xf = x.astype(jnp.float32)
    mu = jnp.mean(xf, axis=-1, keepdims=True)
    xc = xf - mu
    var = jnp.mean(xc * xc, axis=-1, keepdims=True)
    y = xc * lax.rsqrt(var + LN_EPS) * g.astype(jnp.float32) + b.astype(jnp.float32)
    return y.astype(x.dtype)


def swiglu(x, w_gate, w_up, w_down):
    return (jax.nn.silu(x @ w_gate) * (x @ w_up)) @ w_down


def head_rmsnorm(o, g):
    of = o.astype(jnp.float32)
    of = of * lax.rsqrt(jnp.mean(of * of, axis=-1, keepdims=True) + RMS_EPS)
    b, s, h, d = o.shape
    return (of.reshape(b, s, h * d) * g.astype(jnp.float32)).astype(o.dtype)


def stick_breaking_attention(q, k, v):
    seq = q.shape[2]
    scale = 1.0 / math.sqrt(q.shape[-1])
    outs = []
    for i in range(seq // SB_QBLOCK):
        q0 = i * SB_QBLOCK
        q1 = q0 + SB_QBLOCK
        qi = q[:, :, q0:q1]
        kp = k[:, :, :q1]
        vp = v[:, :, :q1]
        z = jnp.einsum('bhqd,bhkd->bhqk', qi, kp).astype(jnp.float32) * scale
        t_pos = q0 + jnp.arange(SB_QBLOCK)[:, None]
        s_pos = jnp.arange(q1)[None, :]
        causal = s_pos < t_pos
        log_keep = jnp.where(causal, jax.nn.log_sigmoid(-z), 0.0)
        log_surv = lax.cumsum(log_keep, axis=3, reverse=True) - log_keep
        w = jnp.where(causal, jnp.exp(jax.nn.log_sigmoid(z) + log_surv), 0.0)
        outs.append(jnp.einsum('bhqk,bhkd->bhqd', w.astype(v.dtype), vp))
    return jnp.concatenate(outs, axis=2)


def chunked_rel_attention(q, k, v, rel_bias):
    b, seq, h, d = q.shape
    n_chunks = seq // CHUNK
    pad = N_PREV_CHUNKS * CHUNK
    scale = 1.0 / math.sqrt(d)
    k_pad = jnp.pad(k, ((0, 0), (pad, 0), (0, 0), (0, 0)))
    v_pad = jnp.pad(v, ((0, 0), (pad, 0), (0, 0), (0, 0)))
    qi = jnp.arange(CHUNK)[:, None]
    kj = jnp.arange(BAND)[None, :]
    rel = qi + pad - kj
    bias = rel_bias[:, jnp.clip(rel, -MAX_REL, MAX_REL) + MAX_REL].astype(jnp.float32)
    neg = jnp.finfo(jnp.float32).min

    def one_chunk(c):
        start = c * CHUNK
        qc = lax.dynamic_slice_in_dim(q, start, CHUNK, axis=1)
        kc = lax.dynamic_slice_in_dim(k_pad, start, BAND, axis=1)
        vc = lax.dynamic_slice_in_dim(v_pad, start, BAND, axis=1)
        sc = jnp.einsum('bqhd,bkhd->bhqk', qc, kc).astype(jnp.float32) * scale + bias
        valid = (start - pad + jnp.arange(BAND)) >= 0
        sc = jnp.where(valid[None, None, None, :], sc, neg)
        p = jax.nn.softmax(sc, axis=-1).astype(v.dtype)
        return jnp.einsum('bhqk,bkhd->bqhd', p, vc)

    out = lax.map(one_chunk, jnp.arange(n_chunks))
    return jnp.moveaxis(out, 0, 1).reshape(b, seq, h, d)


def setup_inputs(seed: int = 0) -> dict:
    key = jax.random.key(seed)
    ks = jax.random.split(key, 32)
    f32 = jnp.float32
    L, D, F = DEPTH, D_MODEL, D_FF
    std_d = D ** -0.5
    std_f = F ** -0.5

    def nrm(k, shape, std):
        return jax.random.normal(k, shape, f32) * std

    def gain(k, shape):
        return 1.0 + 0.02 * jax.random.normal(k, shape, f32)

    x = jax.random.normal(ks[0], (BATCH, SEQ, D), f32)
    w_in = jnp.concatenate([
        nrm(ks[1], (L, D, D_SB), std_d),
        nrm(ks[2], (L, D, D_SB), std_d),
        nrm(ks[3], (L, D, D_SB), std_d * BETA_INIT),
        nrm(ks[4], (L, D, D_CA), std_d),
        nrm(ks[5], (L, D, D_CA), std_d),
        nrm(ks[6], (L, D, D_CA), std_d * BETA_INIT),
    ], axis=-1)
    return {
        "x": x,
        "ffn1_w_gate": nrm(ks[7], (L, D, F), std_d),
        "ffn1_w_up": nrm(ks[8], (L, D, F), std_d),
        "ffn1_w_down": nrm(ks[9], (L, F, D), std_f * BETA_INIT),
        "ln1_g": gain(ks[10], (L, D)),
        "ln1_b": 0.02 * jax.random.normal(ks[11], (L, D), f32),
        "w_in": w_in,
        "rel_bias": 0.1 * jax.random.normal(ks[12], (L, H_CA, 2 * MAX_REL + 1), f32),
        "sb_out_g": gain(ks[13], (L, D_SB)),
        "ca_out_g": gain(ks[14], (L, D_CA)),
        "w_out": nrm(ks[15], (L, MIX_WIDTH, D), MIX_WIDTH ** -0.5 * BETA_INIT),
        "ln2_g": gain(ks[16], (L, D)),
        "ln2_b": 0.02 * jax.random.normal(ks[17], (L, D), f32),
        "ffn2_w_gate": nrm(ks[18], (L, D, F), std_d),
        "ffn2_w_up": nrm(ks[19], (L, D, F), std_d),
        "ffn2_w_down": nrm(ks[20], (L, F, D), std_f * BETA_INIT),
        "ln3_g": gain(ks[21], (L, D)),
        "ln3_b": 0.02 * jax.random.normal(ks[22], (L, D), f32),
    }


def reference(x, ffn1_w_gate, ffn1_w_up, ffn1_w_down, ln1_g, ln1_b, w_in, rel_bias,
              sb_out_g, ca_out_g, w_out, ln2_g, ln2_b, ffn2_w_gate, ffn2_w_up,
              ffn2_w_down, ln3_g, ln3_b):
    b, seq, _ = x.shape
    cuts = [D_SB, 2 * D_SB, 3 * D_SB, 3 * D_SB + D_CA, 3 * D_SB + 2 * D_CA]
    for l in range(DEPTH):
        x = layer_norm(ALPHA * x + FFN_RESIDUAL * swiglu(x, ffn1_w_gate[l], ffn1_w_up[l], ffn1_w_down[l]),
                       ln1_g[l], ln1_b[l])
        h = x @ w_in[l]
        qa, ka, va, qb, kb, vb = jnp.split(h, cuts, axis=-1)
        to_bhsd = lambda t, nh: t.reshape(b, seq, nh, HEAD_DIM).transpose(0, 2, 1, 3)
        oa = stick_breaking_attention(to_bhsd(qa, H_SB), to_bhsd(ka, H_SB), to_bhsd(va, H_SB))
        oa = oa.transpose(0, 2, 1, 3)
        to_bshd = lambda t, nh: t.reshape(b, seq, nh, HEAD_DIM)
        ob = chunked_rel_attention(to_bshd(qb, H_CA), to_bshd(kb, H_CA), to_bshd(vb, H_CA), rel_bias[l])
        mixed = jnp.concatenate([head_rmsnorm(oa, sb_out_g[l]), head_rmsnorm(ob, ca_out_g[l])], axis=-1)
        x = layer_norm(ALPHA * x + mixed @ w_out[l], ln2_g[l], ln2_b[l])
        x = layer_norm(ALPHA * x + FFN_RESIDUAL * swiglu(x, ffn2_w_gate[l], ffn2_w_up[l], ffn2_w_down[l]),
                       ln3_g[l], ln3_b[l])
    return x
```

```python
import functools
import math

import numpy as np
import jax
import jax.numpy as jnp
from jax import lax
from jax.experimental import pallas as pl
from jax.experimental.pallas import tpu as pltpu

D_MODEL = 1024
DEPTH = 4
HEAD_DIM = 64
D_SB = 512
D_CA = 512
CHUNK = 64
N_PREV_CHUNKS = 8
MAX_REL = 128
D_FF = 2816
FFN_RESIDUAL = 0.5
ALPHA = (2 * DEPTH) ** 0.25
LN_EPS = 1e-5
RMS_EPS = 1e-6
ATTN_SCALE = 1.0 / math.sqrt(HEAD_DIM)

LANES = 128
HEADS_PER_BLOCK = LANES // HEAD_DIM
VMEM_LIMIT_BYTES = 56 * 1024 * 1024

ROW_TILE = 512
FF_TILE = 256
N_FF_TILES = D_FF // FF_TILE
SB_BLOCK = 128
CA_PAIR = 2 * CHUNK
CA_PAD = N_PREV_CHUNKS * CHUNK
CA_WINDOW = CA_PAD + CA_PAIR
SB_SKIP_LOG = 104.0
MASK_BIAS = -1e30

_F32 = jnp.float32
_BF16 = jnp.bfloat16


def _layer_norm(y, g, b):
    mu = jnp.mean(y, axis=-1, keepdims=True)
    yc = y - mu
    var = jnp.mean(yc * yc, axis=-1, keepdims=True)
    return yc * lax.rsqrt(var + LN_EPS) * g + b


def _dot(a, b):
    return jnp.dot(a, b, preferred_element_type=_F32)


def _dot_nt(a, b):
    return lax.dot_general(a, b, (((1,), (1,)), ((), ())), preferred_element_type=_F32)


def _split_bf16(a):
    hi = a.astype(_BF16)
    lo = (a - hi.astype(_F32)).astype(_BF16)
    return jnp.concatenate([hi, lo], axis=1)


def _softplus(z):
    return jnp.maximum(z, 0.0) + jnp.log(1.0 + jnp.exp(-jnp.abs(z)))


def _head_rms_gain(o, head0, gain):
    ss = o * o
    s0 = jnp.sum(jnp.where(head0, ss, 0.0), axis=1, keepdims=True)
    s1 = jnp.sum(jnp.where(head0, 0.0, ss), axis=1, keepdims=True)
    r = jnp.where(head0, lax.rsqrt(s0 / HEAD_DIM + RMS_EPS), lax.rsqrt(s1 / HEAD_DIM + RMS_EPS))
    return o * r * gain


def _ffn_ln_kernel(x_ref, wg_ref, wu_ref, wd_ref, g_ref, b_ref, o_ref, xb_ref, acc_ref):
    xb_ref[...] = x_ref[...].astype(_BF16)
    acc_ref[...] = jnp.zeros_like(acc_ref)

    def ff_tile(f, carry):
        xb = xb_ref[...]
        gate = _dot(xb, wg_ref[f])
        up = _dot(xb, wu_ref[f])
        hidden = (gate * jax.nn.sigmoid(gate) * up).astype(_BF16)
        acc_ref[...] += _dot(hidden, wd_ref[f])
        return carry

    lax.fori_loop(0, N_FF_TILES, ff_tile, 0)
    y = ALPHA * x_ref[...] + FFN_RESIDUAL * acc_ref[...]
    o_ref[...] = _layer_norm(y, g_ref[...], b_ref[...])


def _ffn_ln(x, wg, wu, wd, g, b):
    rows = x.shape[0]
    assert rows % ROW_TILE == 0
    const3 = lambda i: (0, 0, 0)
    const2 = lambda i: (0, 0)
    return pl.pallas_call(
        _ffn_ln_kernel,
        grid=(rows // ROW_TILE,),
        in_specs=[
            pl.BlockSpec((ROW_TILE, D_MODEL), lambda i: (i, 0)),
            pl.BlockSpec((N_FF_TILES, D_MODEL, FF_TILE), const3),
            pl.BlockSpec((N_FF_TILES, D_MODEL, FF_TILE), const3),
            pl.BlockSpec((N_FF_TILES, FF_TILE, D_MODEL), const3),
            pl.BlockSpec((1, D_MODEL), const2),
            pl.BlockSpec((1, D_MODEL), const2),
        ],
        out_specs=pl.BlockSpec((ROW_TILE, D_MODEL), lambda i: (i, 0)),
        out_shape=jax.ShapeDtypeStruct((rows, D_MODEL), _F32),
        scratch_shapes=[
            pltpu.VMEM((ROW_TILE, D_MODEL), _BF16),
            pltpu.VMEM((ROW_TILE, D_MODEL), _F32),
        ],
        compiler_params=pltpu.CompilerParams(
            dimension_semantics=("arbitrary",), vmem_limit_bytes=VMEM_LIMIT_BYTES),
        name="ffn_ln",
    )(x, wg, wu, wd, g, b)


IN_PROJ_COLS = 3 * (D_SB + D_CA)
IN_PROJ_TILE = 512


def _in_proj_kernel(x_ref, w_ref, o_ref):
    xb = x_ref[...].astype(_BF16)
    for c in range(IN_PROJ_COLS // IN_PROJ_TILE):
        cols = slice(c * IN_PROJ_TILE, (c + 1) * IN_PROJ_TILE)
        o_ref[:, cols] = _dot(xb, w_ref[:, cols]).astype(_BF16)


def _in_proj(x, w):
    rows = x.shape[0]
    return pl.pallas_call(
        _in_proj_kernel,
        grid=(rows // ROW_TILE,),
        in_specs=[
            pl.BlockSpec((ROW_TILE, D_MODEL), lambda i: (i, 0)),
            pl.BlockSpec((D_MODEL, IN_PROJ_COLS), lambda i: (0, 0)),
        ],
        out_specs=pl.BlockSpec((ROW_TILE, IN_PROJ_COLS), lambda i: (i, 0)),
        out_shape=jax.ShapeDtypeStruct((rows, IN_PROJ_COLS), _BF16),
        compiler_params=pltpu.CompilerParams(
            dimension_semantics=("arbitrary",), vmem_limit_bytes=VMEM_LIMIT_BYTES),
        name="in_proj",
    )(x, w)


def _sb_cumsum_rhs(n_keys):
    j = np.arange(n_keys)[:, None]
    s = np.arange(n_keys)[None, :]
    u = np.concatenate([(j >= s), np.ones((n_keys, LANES), bool)], axis=1).astype(np.float32)
    return jnp.asarray(np.concatenate([u, u], axis=0), dtype=_BF16)


def _sb_kernel(q_ref, k_ref, v_ref, u2_ref, u1_ref, gain_ref, o_ref, kpad_ref, vpad_ref):
    seq = q_ref.shape[0]
    n_blocks = seq // SB_BLOCK
    kpad_ref[0:SB_BLOCK, :] = jnp.zeros((SB_BLOCK, LANES), _BF16)
    vpad_ref[0:SB_BLOCK, :] = jnp.zeros((SB_BLOCK, LANES), _BF16)
    kpad_ref[SB_BLOCK:, :] = k_ref[...]
    vpad_ref[SB_BLOCK:, :] = v_ref[...]

    lane = lax.broadcasted_iota(jnp.int32, (SB_BLOCK, LANES), 1)
    head0 = lane < HEAD_DIM
    row2 = lax.broadcasted_iota(jnp.int32, (SB_BLOCK, 2 * SB_BLOCK), 0)
    col2 = lax.broadcasted_iota(jnp.int32, (SB_BLOCK, 2 * SB_BLOCK), 1)
    causal2 = col2 < row2 + SB_BLOCK
    gain = gain_ref[...]

    def q_block(i, carry):
        r0 = pl.multiple_of(i * SB_BLOCK, SB_BLOCK)
        q = q_ref[pl.ds(r0, SB_BLOCK), :].astype(_F32) * ATTN_SCALE
        kw = kpad_ref[pl.ds(r0, 2 * SB_BLOCK), :]
        vw = vpad_ref[pl.ds(r0, 2 * SB_BLOCK), :]
        outs = []
        for hd in range(HEADS_PER_BLOCK):
            mine = head0 if hd == 0 else jnp.logical_not(head0)
            qh = jnp.where(mine, q, 0.0).astype(_BF16)
            z = _dot_nt(qh, kw)
            sp = jnp.where(causal2, _softplus(z), 0.0)
            cs = _dot(_split_bf16(sp), u2_ref[...])
            w = jnp.where(causal2, jnp.exp(z - cs[:, :2 * SB_BLOCK]), 0.0)
            acc = _dot(w.astype(_BF16), vw)
            spent = cs[:, 2 * SB_BLOCK:]

            def more(state):
                j, _, spent = state
                return jnp.logical_and(j >= 0, jnp.min(spent) <= SB_SKIP_LOG)

            def key_block(state):
                j, acc, spent = state
                rr = pl.multiple_of((j + 1) * SB_BLOCK, SB_BLOCK)
                kb = kpad_ref[pl.ds(rr, SB_BLOCK), :]
                vb = vpad_ref[pl.ds(rr, SB_BLOCK), :]
                z = _dot_nt(qh, kb)
                cs = _dot(_split_bf16(_softplus(z)), u1_ref[...])
                w = jnp.exp(z - (cs[:, :SB_BLOCK] + spent))
                return j - 1, acc + _dot(w.astype(_BF16), vb), spent + cs[:, SB_BLOCK:]

            _, acc, _ = lax.while_loop(more, key_block, (i - 2, acc, spent))
            outs.append(acc)
        o = jnp.where(head0, outs[0], outs[1])
        o_ref[pl.ds(r0, SB_BLOCK), :] = _head_rms_gain(o, head0, gain).astype(_BF16)
        return carry

    lax.fori_loop(0, n_blocks, q_block, 0)


def _sb_attn(h, gain):
    batch, seq, _ = h.shape
    n_pairs = D_SB // LANES
    col_blocks = D_SB // LANES
    u2 = _sb_cumsum_rhs(2 * SB_BLOCK)
    u1 = _sb_cumsum_rhs(SB_BLOCK)
    return pl.pallas_call(
        _sb_kernel,
        grid=(batch, n_pairs),
        in_specs=[
            pl.BlockSpec((None, seq, LANES), lambda b, p: (b, 0, p)),
            pl.BlockSpec((None, seq, LANES), lambda b, p: (b, 0, col_blocks + p)),
            pl.BlockSpec((None, seq, LANES), lambda b, p: (b, 0, 2 * col_blocks + p)),
            pl.BlockSpec(u2.shape, lambda b, p: (0, 0)),
            pl.BlockSpec(u1.shape, lambda b, p: (0, 0)),
            pl.BlockSpec((1, LANES), lambda b, p: (0, p)),
        ],
        out_specs=pl.BlockSpec((None, seq, LANES), lambda b, p: (b, 0, p)),
        out_shape=jax.ShapeDtypeStruct((batch, seq, D_SB), _BF16),
        scratch_shapes=[
            pltpu.VMEM((seq + SB_BLOCK, LANES), _BF16),
            pltpu.VMEM((seq + SB_BLOCK, LANES), _BF16),
        ],
        compiler_params=pltpu.CompilerParams(
            dimension_semantics=("arbitrary", "arbitrary"), vmem_limit_bytes=VMEM_LIMIT_BYTES),
        name="sb_attn",
    )(h, h, h, u2, u1, gain)


def _ca_bias_table(rel_bias):
    qi = np.arange(CA_PAIR)[:, None]
    kj = np.arange(CA_WINDOW)[None, :]
    rel = np.clip(qi + CA_PAD - kj, -MAX_REL, MAX_REL) + MAX_REL
    first = qi < CHUNK
    visible = np.where(first, kj < CA_PAD + CHUNK, kj >= CHUNK)
    table = rel_bias[:, rel].astype(_F32)
    return jnp.where(jnp.asarray(visible)[None], table, MASK_BIAS)


def _ca_kernel(q_ref, k_ref, v_ref, bias_ref, gain_ref, o_ref, kpad_ref, vpad_ref):
    seq = q_ref.shape[0]
    n_pairs = seq // CA_PAIR
    kpad_ref[0:CA_PAD, :] = jnp.zeros((CA_PAD, LANES), _BF16)
    vpad_ref[0:CA_PAD, :] = jnp.zeros((CA_PAD, LANES), _BF16)
    kpad_ref[CA_PAD:, :] = k_ref[...]
    vpad_ref[CA_PAD:, :] = v_ref[...]

    lane = lax.broadcasted_iota(jnp.int32, (CA_PAIR, LANES), 1)
    head0 = lane < HEAD_DIM
    col = lax.broadcasted_iota(jnp.int32, (CA_PAIR, CA_WINDOW), 1)
    gain = gain_ref[...]

    def pair(c2, before_start):
        r0 = pl.multiple_of(c2 * CA_PAIR, CA_PAIR)
        q = q_ref[pl.ds(r0, CA_PAIR), :].astype(_F32) * ATTN_SCALE
        kw = kpad_ref[pl.ds(r0, CA_WINDOW), :]
        vw = vpad_ref[pl.ds(r0, CA_WINDOW), :]
        outs = []
        for hd in range(HEADS_PER_BLOCK):
            mine = head0 if hd == 0 else jnp.logical_not(head0)
            qh = jnp.where(mine, q, 0.0).astype(_BF16)
            sc = _dot_nt(qh, kw) + bias_ref[hd]
            if before_start:
                sc = jnp.where(col >= CA_PAD - r0, sc, MASK_BIAS)
            m = jnp.max(sc, axis=1, keepdims=True)
            p = jnp.exp(sc - m)
            denom = jnp.sum(p, axis=1, keepdims=True)
            outs.append(_dot(p.astype(_BF16), vw) / denom)
        o = jnp.where(head0, outs[0], outs[1])
        o_ref[pl.ds(r0, CA_PAIR), :] = _head_rms_gain(o, head0, gain).astype(_BF16)

    n_edge = min(CA_PAD // CA_PAIR, n_pairs)

    def edge_pair(c2, carry):
        pair(c2, True)
        return carry

    def inner_pair(c2, carry):
        pair(c2, False)
        return carry

    lax.fori_loop(0, n_edge, edge_pair, 0)
    lax.fori_loop(n_edge, n_pairs, inner_pair, 0)


def _ca_attn(h, bias, gain):
    batch, seq, _ = h.shape
    n_pairs = D_CA // LANES
    first = 3 * D_SB // LANES
    col_blocks = D_CA // LANES
    return pl.pallas_call(
        _ca_kernel,
        grid=(batch, n_pairs),
        in_specs=[
            pl.BlockSpec((None, seq, LANES), lambda b, p: (b, 0, first + p)),
            pl.BlockSpec((None, seq, LANES), lambda b, p: (b, 0, first + col_blocks + p)),
            pl.BlockSpec((None, seq, LANES), lambda b, p: (b, 0, first + 2 * col_blocks + p)),
            pl.BlockSpec((HEADS_PER_BLOCK, CA_PAIR, CA_WINDOW), lambda b, p: (p, 0, 0)),
            pl.BlockSpec((1, LANES), lambda b, p: (0, p)),
        ],
        out_specs=pl.BlockSpec((None, seq, LANES), lambda b, p: (b, 0, p)),
        out_shape=jax.ShapeDtypeStruct((batch, seq, D_CA), _BF16),
        scratch_shapes=[
            pltpu.VMEM((seq + CA_PAD, LANES), _BF16),
            pltpu.VMEM((seq + CA_PAD, LANES), _BF16),
        ],
        compiler_params=pltpu.CompilerParams(
            dimension_semantics=("arbitrary", "arbitrary"), vmem_limit_bytes=VMEM_LIMIT_BYTES),
        name="ca_attn",
    )(h, h, h, bias, gain)


def _out_proj_ln_kernel(x_ref, ma_ref, mb_ref, w_ref, g_ref, b_ref, o_ref):
    y = _dot(ma_ref[...], w_ref[0:D_SB, :]) + _dot(mb_ref[...], w_ref[D_SB:, :])
    o_ref[...] = _layer_norm(ALPHA * x_ref[...] + y, g_ref[...], b_ref[...])


def _out_proj_ln(x, ma, mb, w, g, b):
    rows = x.shape[0]
    const2 = lambda i: (0, 0)
    return pl.pallas_call(
        _out_proj_ln_kernel,
        grid=(rows // ROW_TILE,),
        in_specs=[
            pl.BlockSpec((ROW_TILE, D_MODEL), lambda i: (i, 0)),
            pl.BlockSpec((ROW_TILE, D_SB), lambda i: (i, 0)),
            pl.BlockSpec((ROW_TILE, D_CA), lambda i: (i, 0)),
            pl.BlockSpec((D_SB + D_CA, D_MODEL), const2),
            pl.BlockSpec((1, D_MODEL), const2),
            pl.BlockSpec((1, D_MODEL), const2),
        ],
        out_specs=pl.BlockSpec((ROW_TILE, D_MODEL), lambda i: (i, 0)),
        out_shape=jax.ShapeDtypeStruct((rows, D_MODEL), _F32),
        compiler_params=pltpu.CompilerParams(
            dimension_semantics=("arbitrary",), vmem_limit_bytes=VMEM_LIMIT_BYTES),
        name="out_proj_ln",
    )(x, ma, mb, w, g, b)


def _ff_tiles_in(w):
    return w.astype(_BF16).reshape(D_MODEL, N_FF_TILES, FF_TILE).transpose(1, 0, 2)


def _ff_tiles_out(w):
    return w.astype(_BF16).reshape(N_FF_TILES, FF_TILE, D_MODEL)


def kernel(x, ffn1_w_gate, ffn1_w_up, ffn1_w_down, ln1_g, ln1_b, w_in, rel_bias, sb_out_g, ca_out_g, w_out,
           ln2_g, ln2_b, ffn2_w_gate, ffn2_w_up, ffn2_w_down, ln3_g, ln3_b):
    batch, seq, _ = x.shape
    assert seq % CA_PAIR == 0 and seq % SB_BLOCK == 0 and (batch * seq) % ROW_TILE == 0
    row = lambda a: a.reshape(1, -1)
    xf = x.reshape(batch * seq, D_MODEL)
    for l in range(DEPTH):
        xf = _ffn_ln(xf, _ff_tiles_in(ffn1_w_gate[l]), _ff_tiles_in(ffn1_w_up[l]), _ff_tiles_out(ffn1_w_down[l]),
                     row(ln1_g[l]), row(ln1_b[l]))
        h = _in_proj(xf, w_in[l].astype(_BF16)).reshape(batch, seq, IN_PROJ_COLS)
        ma = _sb_attn(h, row(sb_out_g[l]))
        mb = _ca_attn(h, _ca_bias_table(rel_bias[l]), row(ca_out_g[l]))
        xf = _out_proj_ln(xf, ma.reshape(batch * seq, D_SB), mb.reshape(batch * seq, D_CA),
                          w_out[l].astype(_BF16), row(ln2_g[l]), row(ln2_b[l]))
        xf = _ffn_ln(xf, _ff_tiles_in(ffn2_w_gate[l]), _ff_tiles_in(ffn2_w_up[l]), _ff_tiles_out(ffn2_w_down[l]),
                     row(ln3_g[l]), row(ln3_b[l]))
    return xf.reshape(batch, seq, D_MODEL)
```

```python
import math

import numpy as np
import jax
import jax.numpy as jnp
from jax import lax
from jax.experimental import pallas as pl
from jax.experimental.pallas import tpu as pltpu

D_MODEL = 1024
DEPTH = 4
HEAD_DIM = 64
D_SB = 512
D_CA = 512
CHUNK = 64
N_PREV_CHUNKS = 8
MAX_REL = 128
D_FF = 2816
FFN_RESIDUAL = 0.5
ALPHA = (2 * DEPTH) ** 0.25
LN_EPS = 1e-5
RMS_EPS = 1e-6
ATTN_SCALE = 1.0 / math.sqrt(HEAD_DIM)

LANES = 128
HEADS_PER_BLOCK = LANES // HEAD_DIM
VMEM_LIMIT_BYTES = 56 * 1024 * 1024

ROW_TILE = 512
FF_TILE = 256
N_FF_TILES = D_FF // FF_TILE
SB_QBLOCK = 64
SB_KBLOCK = LANES
SB_WINDOW = 2 * SB_KBLOCK
SB_PAD = SB_WINDOW - SB_QBLOCK
SB_GROUP = 8
CA_GROUP = 4
CA_PAIR = 2 * CHUNK
CA_PAD = N_PREV_CHUNKS * CHUNK
CA_WINDOW = CA_PAD + CA_PAIR
SB_SKIP_LOG = 104.0
MASK_BIAS = -1e30

_F32 = jnp.float32
_BF16 = jnp.bfloat16


def _layer_norm(y, g, b):
    mu = jnp.mean(y, axis=-1, keepdims=True)
    yc = y - mu
    var = jnp.mean(yc * yc, axis=-1, keepdims=True)
    return yc * lax.rsqrt(var + LN_EPS) * g + b


def _dot(a, b):
    return jnp.dot(a, b, preferred_element_type=_F32)


def _dot_nt(a, b):
    return lax.dot_general(a, b, (((1,), (1,)), ((), ())), preferred_element_type=_F32)


def _split_bf16(a):
    hi = lax.bitcast_convert_type(lax.bitcast_convert_type(a, jnp.uint32) & jnp.uint32(0xFFFF0000), _F32)
    return jnp.concatenate([hi.astype(_BF16), (a - hi).astype(_BF16)], axis=1)


def _softplus(z):
    return jnp.maximum(z, 0.0) + jnp.log(1.0 + jnp.exp(-jnp.abs(z)))


def _head_rms_gain(o, head0, gain):
    ss = o * o
    s0 = jnp.sum(jnp.where(head0, ss, 0.0), axis=1, keepdims=True)
    s1 = jnp.sum(jnp.where(head0, 0.0, ss), axis=1, keepdims=True)
    r = jnp.where(head0, lax.rsqrt(s0 / HEAD_DIM + RMS_EPS), lax.rsqrt(s1 / HEAD_DIM + RMS_EPS))
    return o * r * gain


def _ffn_ln_kernel(x_ref, wg_ref, wu_ref, wd_ref, g_ref, b_ref, o_ref, xb_ref, acc_ref):
    xb_ref[...] = x_ref[...].astype(_BF16)
    for f in range(N_FF_TILES):
        xb = xb_ref[...]
        gate = _dot(xb, wg_ref[f])
        up = _dot(xb, wu_ref[f])
        hidden = (gate * jax.nn.sigmoid(gate) * up).astype(_BF16)
        part = _dot(hidden, wd_ref[f])
        if f == 0:
            acc_ref[...] = part
        else:
            acc_ref[...] += part
    y = ALPHA * x_ref[...] + acc_ref[...]
    o_ref[...] = _layer_norm(y, g_ref[...], b_ref[...])


def _ffn_ln(x, wg, wu, wd, g, b):
    rows = x.shape[0]
    assert rows % ROW_TILE == 0
    const3 = lambda i: (0, 0, 0)
    const2 = lambda i: (0, 0)
    return pl.pallas_call(
        _ffn_ln_kernel,
        grid=(rows // ROW_TILE,),
        in_specs=[
            pl.BlockSpec((ROW_TILE, D_MODEL), lambda i: (i, 0)),
            pl.BlockSpec((N_FF_TILES, D_MODEL, FF_TILE), const3),
            pl.BlockSpec((N_FF_TILES, D_MODEL, FF_TILE), const3),
            pl.BlockSpec((N_FF_TILES, FF_TILE, D_MODEL), const3),
            pl.BlockSpec((1, D_MODEL), const2),
            pl.BlockSpec((1, D_MODEL), const2),
        ],
        out_specs=pl.BlockSpec((ROW_TILE, D_MODEL), lambda i: (i, 0)),
        out_shape=jax.ShapeDtypeStruct((rows, D_MODEL), _F32),
        scratch_shapes=[
            pltpu.VMEM((ROW_TILE, D_MODEL), _BF16),
            pltpu.VMEM((ROW_TILE, D_MODEL), _F32),
        ],
        compiler_params=pltpu.CompilerParams(
            dimension_semantics=("arbitrary",), vmem_limit_bytes=VMEM_LIMIT_BYTES),
        name="ffn_ln",
    )(x, wg, wu, wd, g, b)


IN_PROJ_COLS = 3 * (D_SB + D_CA)
IN_PROJ_TILE = 512


def _in_proj_kernel(x_ref, w_ref, o_ref):
    xb = x_ref[...].astype(_BF16)
    for c in range(IN_PROJ_COLS // IN_PROJ_TILE):
        cols = slice(c * IN_PROJ_TILE, (c + 1) * IN_PROJ_TILE)
        o_ref[:, cols] = _dot(xb, w_ref[:, cols]).astype(_BF16)


def _in_proj(x, w):
    rows = x.shape[0]
    return pl.pallas_call(
        _in_proj_kernel,
        grid=(rows // ROW_TILE,),
        in_specs=[
            pl.BlockSpec((ROW_TILE, D_MODEL), lambda i: (i, 0)),
            pl.BlockSpec((D_MODEL, IN_PROJ_COLS), lambda i: (0, 0)),
        ],
        out_specs=pl.BlockSpec((ROW_TILE, IN_PROJ_COLS), lambda i: (i, 0)),
        out_shape=jax.ShapeDtypeStruct((rows, IN_PROJ_COLS), _BF16),
        compiler_params=pltpu.CompilerParams(
            dimension_semantics=("arbitrary",), vmem_limit_bytes=VMEM_LIMIT_BYTES),
        name="in_proj",
    )(x, w)


def _sb_cumsum_rhs():
    j = np.arange(SB_KBLOCK)[:, None]
    s = np.arange(SB_KBLOCK)[None, :]
    u = np.concatenate([(j >= s), np.ones((SB_KBLOCK, LANES), bool)], axis=1).astype(np.float32)
    return jnp.asarray(np.concatenate([u, u], axis=0), dtype=_BF16)


def _sb_kernel(q_ref, k_ref, v_ref, uu_ref, gain_ref, o_ref, kpad_ref, vpad_ref, spent_ref):
    seq = q_ref.shape[0]
    n_qblocks = seq // SB_QBLOCK
    kpad_ref[0:SB_PAD, :] = jnp.zeros((SB_PAD, LANES), _BF16)
    vpad_ref[0:SB_PAD, :] = jnp.zeros((SB_PAD, LANES), _BF16)
    kpad_ref[SB_PAD:, :] = k_ref[...]
    vpad_ref[SB_PAD:, :] = v_ref[...]

    rows2 = 2 * SB_QBLOCK
    row = lax.broadcasted_iota(jnp.int32, (rows2, LANES), 0)
    lane = lax.broadcasted_iota(jnp.int32, (rows2, LANES), 1)
    own_scale = jnp.where((row < SB_QBLOCK) == (lane < HEAD_DIM), ATTN_SCALE, 0.0).astype(_BF16)
    visible = lane - (SB_KBLOCK - SB_QBLOCK) < (row & (SB_QBLOCK - 1))
    head0 = lax.broadcasted_iota(jnp.int32, (SB_QBLOCK, LANES), 1) < HEAD_DIM
    gain = gain_ref[...]

    def windows(ms):
        r0s = [pl.multiple_of(m * SB_QBLOCK, SB_QBLOCK) for m in ms]
        q2s, zs, sps = [], [], []
        for r0 in r0s:
            q = q_ref[pl.ds(r0, SB_QBLOCK), :]
            q2s.append(jnp.concatenate([q, q], axis=0) * own_scale)
        for r0, q2 in zip(r0s, q2s):
            zs.append(_dot_nt(q2, kpad_ref[pl.ds(r0, SB_WINDOW), :]))
        zs = [(z[:, :SB_KBLOCK], jnp.where(visible, z[:, SB_KBLOCK:], MASK_BIAS)) for z in zs]
        for z_far, z_near in zs:
            sps.append(_softplus(z_far))
            sps.append(_softplus(z_near))
        cs = _dot(_split_bf16(jnp.concatenate(sps, axis=0)), uu_ref[...])
        ws, spents = [], []
        for g, (z_far, z_near) in enumerate(zs):
            far = cs[(2 * g) * rows2:(2 * g + 1) * rows2]
            near = cs[(2 * g + 1) * rows2:(2 * g + 2) * rows2]
            tot_near = near[:, SB_KBLOCK:]
            w_far = jnp.exp(z_far - (far[:, :SB_KBLOCK] + tot_near))
            w_near = jnp.exp(z_near - near[:, :SB_KBLOCK])
            ws.append(jnp.concatenate([w_far, w_near], axis=1).astype(_BF16))
            spents.append(far[:, SB_KBLOCK:] + tot_near)
        accs = [_dot(w, vpad_ref[pl.ds(r0, SB_WINDOW), :]) for r0, w in zip(r0s, ws)]
        return r0s, q2s, accs, spents

    def finish(r0, acc):
        o = jnp.where(head0, acc[:SB_QBLOCK], acc[SB_QBLOCK:])
        o_ref[pl.ds(r0, SB_QBLOCK), :] = _head_rms_gain(o, head0, gain).astype(_BF16)

    def first_pass(it, carry):
        ms = [it * SB_GROUP + g for g in range(SB_GROUP)]
        r0s, _, accs, spents = windows(ms)
        for m, r0, acc, spent in zip(ms, r0s, accs, spents):
            finish(r0, acc)
            spent_ref[m] = jnp.min(spent)
        return carry

    lax.fori_loop(0, n_qblocks // SB_GROUP, first_pass, 0)

    def second_pass(m, carry):
        @pl.when(spent_ref[m] <= SB_SKIP_LOG)
        def _():
            (r0,), (q2,), (acc,), (spent,) = windows([m])

            def more(state):
                j, _, spent = state
                return jnp.logical_and(r0 - j * SB_KBLOCK > SB_PAD, jnp.min(spent) <= SB_SKIP_LOG)

            def key_block(state):
                j, acc, spent = state
                rr = pl.multiple_of(r0 - (j + 1) * SB_KBLOCK, SB_QBLOCK)
                kb = kpad_ref[pl.ds(rr, SB_KBLOCK), :]
                vb = vpad_ref[pl.ds(rr, SB_KBLOCK), :]
                z = _dot_nt(q2, kb)
                cs = _dot(_split_bf16(_softplus(z)), uu_ref[...])
                w = jnp.exp(z - (cs[:, :SB_KBLOCK] + spent))
                return j + 1, acc + _dot(w.astype(_BF16), vb), spent + cs[:, SB_KBLOCK:]

            _, acc, _ = lax.while_loop(more, key_block, (0, acc, spent))
            finish(r0, acc)
        return carry

    lax.fori_loop(0, n_qblocks, second_pass, 0)


def _sb_attn(h, gain):
    batch, seq, _ = h.shape
    n_pairs = D_SB // LANES
    col_blocks = D_SB // LANES
    uu = _sb_cumsum_rhs()
    return pl.pallas_call(
        _sb_kernel,
        grid=(batch, n_pairs),
        in_specs=[
            pl.BlockSpec((None, seq, LANES), lambda b, p: (b, 0, p)),
            pl.BlockSpec((None, seq, LANES), lambda b, p: (b, 0, col_blocks + p)),
            pl.BlockSpec((None, seq, LANES), lambda b, p: (b, 0, 2 * col_blocks + p)),
            pl.BlockSpec(uu.shape, lambda b, p: (0, 0)),
            pl.BlockSpec((1, LANES), lambda b, p: (0, p)),
        ],
        out_specs=pl.BlockSpec((None, seq, LANES), lambda b, p: (b, 0, p)),
        out_shape=jax.ShapeDtypeStruct((batch, seq, D_SB), _BF16),
        scratch_shapes=[
            pltpu.VMEM((seq + SB_PAD, LANES), _BF16),
            pltpu.VMEM((seq + SB_PAD, LANES), _BF16),
            pltpu.SMEM((seq // SB_QBLOCK,), _F32),
        ],
        compiler_params=pltpu.CompilerParams(
            dimension_semantics=("arbitrary", "arbitrary"), vmem_limit_bytes=VMEM_LIMIT_BYTES),
        name="sb_attn",
    )(h, h, h, uu, gain)


def _ca_bias_table(rel_bias):
    n_heads = rel_bias.shape[0]
    row_len = CA_WINDOW + CA_PAIR
    period = row_len + 1
    n_far = CA_PAD - MAX_REL + 1
    far = rel_bias[:, 2 * MAX_REL:]
    ramp = rel_bias[:, 1:2 * MAX_REL][:, ::-1]
    by_offset = jnp.concatenate([
        jnp.broadcast_to(far, (n_heads, n_far)), ramp,
        jnp.broadcast_to(far, (n_heads, period - n_far - ramp.shape[1]))], axis=1)
    table = jnp.tile(by_offset, (1, CA_PAIR))[:, :CA_PAIR * row_len]
    table = table.reshape(n_heads, CA_PAIR, row_len)[:, :, :CA_WINDOW]
    qi = np.arange(CA_PAIR)[:, None]
    kj = np.arange(CA_WINDOW)[None, :]
    visible = np.where(qi < CHUNK, kj < CA_PAD + CHUNK, kj >= CHUNK)
    table = jnp.where(jnp.asarray(visible)[None], table.astype(_F32), MASK_BIAS)
    return table.reshape(-1, CA_WINDOW)


def _ca_kernel(q_ref, k_ref, v_ref, bias_ref, gain_ref, o_ref, kpad_ref, vpad_ref):
    seq = q_ref.shape[0]
    n_pairs = seq // CA_PAIR
    kpad_ref[0:CA_PAD, :] = jnp.zeros((CA_PAD, LANES), _BF16)
    vpad_ref[0:CA_PAD, :] = jnp.zeros((CA_PAD, LANES), _BF16)
    kpad_ref[CA_PAD:, :] = k_ref[...]
    vpad_ref[CA_PAD:, :] = v_ref[...]

    rows2 = 2 * CA_PAIR
    row = lax.broadcasted_iota(jnp.int32, (rows2, LANES), 0)
    lane = lax.broadcasted_iota(jnp.int32, (rows2, LANES), 1)
    own_scale = jnp.where((row < CA_PAIR) == (lane < HEAD_DIM), ATTN_SCALE, 0.0).astype(_BF16)
    head0 = lax.broadcasted_iota(jnp.int32, (CA_PAIR, LANES), 1) < HEAD_DIM
    gain = gain_ref[...]

    def pairs(c2s, before_start):
        r0s = [pl.multiple_of(c2 * CA_PAIR, CA_PAIR) for c2 in c2s]
        q2s, scs, ps, denoms = [], [], [], []
        for r0 in r0s:
            q = q_ref[pl.ds(r0, CA_PAIR), :]
            q2s.append(jnp.concatenate([q, q], axis=0) * own_scale)
        for r0, q2 in zip(r0s, q2s):
            sc = _dot_nt(q2, kpad_ref[pl.ds(r0, CA_WINDOW), :]) + bias_ref[...]
            if before_start:
                col = lax.broadcasted_iota(jnp.int32, (rows2, CA_WINDOW), 1)
                sc = jnp.where(col >= CA_PAD - r0, sc, MASK_BIAS)
            scs.append(sc)
        for sc in scs:
            p = jnp.exp(sc - jnp.max(sc, axis=1, keepdims=True))
            denoms.append(jnp.sum(p, axis=1, keepdims=True))
            ps.append(p.astype(_BF16))
        accs = [_dot(p, vpad_ref[pl.ds(r0, CA_WINDOW), :]) for r0, p in zip(r0s, ps)]
        for r0, acc, denom in zip(r0s, accs, denoms):
            acc = acc / denom
            o = jnp.where(head0, acc[:CA_PAIR], acc[CA_PAIR:])
            o_ref[pl.ds(r0, CA_PAIR), :] = _head_rms_gain(o, head0, gain).astype(_BF16)

    n_edge = min(CA_PAD // CA_PAIR, n_pairs)

    def edge_pairs(it, carry):
        pairs([it * CA_GROUP + g for g in range(CA_GROUP)], True)
        return carry

    def inner_pairs(it, carry):
        pairs([it * CA_GROUP + g for g in range(CA_GROUP)], False)
        return carry

    lax.fori_loop(0, n_edge // CA_GROUP, edge_pairs, 0)
    lax.fori_loop(n_edge // CA_GROUP, n_pairs // CA_GROUP, inner_pairs, 0)


def _ca_attn(h, bias, gain):
    batch, seq, _ = h.shape
    n_pairs = D_CA // LANES
    first = 3 * D_SB // LANES
    col_blocks = D_CA // LANES
    return pl.pallas_call(
        _ca_kernel,
        grid=(batch, n_pairs),
        in_specs=[
            pl.BlockSpec((None, seq, LANES), lambda b, p: (b, 0, first + p)),
            pl.BlockSpec((None, seq, LANES), lambda b, p: (b, 0, first + col_blocks + p)),
            pl.BlockSpec((None, seq, LANES), lambda b, p: (b, 0, first + 2 * col_blocks + p)),
            pl.BlockSpec((HEADS_PER_BLOCK * CA_PAIR, CA_WINDOW), lambda b, p: (p, 0)),
            pl.BlockSpec((1, LANES), lambda b, p: (0, p)),
        ],
        out_specs=pl.BlockSpec((None, seq, LANES), lambda b, p: (b, 0, p)),
        out_shape=jax.ShapeDtypeStruct((batch, seq, D_CA), _BF16),
        scratch_shapes=[
            pltpu.VMEM((seq + CA_PAD, LANES), _BF16),
            pltpu.VMEM((seq + CA_PAD, LANES), _BF16),
        ],
        compiler_params=pltpu.CompilerParams(
            dimension_semantics=("arbitrary", "arbitrary"), vmem_limit_bytes=VMEM_LIMIT_BYTES),
        name="ca_attn",
    )(h, h, h, bias, gain)


def _out_proj_ln_kernel(x_ref, ma_ref, mb_ref, w_ref, g_ref, b_ref, o_ref):
    y = _dot(ma_ref[...], w_ref[0:D_SB, :]) + _dot(mb_ref[...], w_ref[D_SB:, :])
    o_ref[...] = _layer_norm(ALPHA * x_ref[...] + y, g_ref[...], b_ref[...])


def _out_proj_ln(x, ma, mb, w, g, b):
    rows = x.shape[0]
    const2 = lambda i: (0, 0)
    return pl.pallas_call(
        _out_proj_ln_kernel,
        grid=(rows // ROW_TILE,),
        in_specs=[
            pl.BlockSpec((ROW_TILE, D_MODEL), lambda i: (i, 0)),
            pl.BlockSpec((ROW_TILE, D_SB), lambda i: (i, 0)),
            pl.BlockSpec((ROW_TILE, D_CA), lambda i: (i, 0)),
            pl.BlockSpec((D_SB + D_CA, D_MODEL), const2),
            pl.BlockSpec((1, D_MODEL), const2),
            pl.BlockSpec((1, D_MODEL), const2),
        ],
        out_specs=pl.BlockSpec((ROW_TILE, D_MODEL), lambda i: (i, 0)),
        out_shape=jax.ShapeDtypeStruct((rows, D_MODEL), _F32),
        compiler_params=pltpu.CompilerParams(
            dimension_semantics=("arbitrary",), vmem_limit_bytes=VMEM_LIMIT_BYTES),
        name="out_proj_ln",
    )(x, ma, mb, w, g, b)


def _ff_tiles_in(w):
    return w.astype(_BF16).reshape(D_MODEL, N_FF_TILES, FF_TILE).transpose(1, 0, 2)


def _ff_tiles_out(w):
    return (FFN_RESIDUAL * w).astype(_BF16).reshape(N_FF_TILES, FF_TILE, D_MODEL)


def kernel(x, ffn1_w_gate, ffn1_w_up, ffn1_w_down, ln1_g, ln1_b, w_in, rel_bias, sb_out_g, ca_out_g, w_out,
           ln2_g, ln2_b, ffn2_w_gate, ffn2_w_up, ffn2_w_down, ln3_g, ln3_b):
    batch, seq, _ = x.shape
    assert seq % (CA_PAIR * CA_GROUP) == 0 and seq % (SB_QBLOCK * SB_GROUP) == 0 and (batch * seq) % ROW_TILE == 0
    row = lambda a: a.reshape(1, -1)
    xf = x.reshape(batch * seq, D_MODEL)
    for l in range(DEPTH):
        xf = _ffn_ln(xf, _ff_tiles_in(ffn1_w_gate[l]), _ff_tiles_in(ffn1_w_up[l]), _ff_tiles_out(ffn1_w_down[l]),
                     row(ln1_g[l]), row(ln1_b[l]))
        h = _in_proj(xf, w_in[l].astype(_BF16)).reshape(batch, seq, IN_PROJ_COLS)
        ma = _sb_attn(h, row(sb_out_g[l]))
        mb = _ca_attn(h, _ca_bias_table(rel_bias[l]), row(ca_out_g[l]))
        xf = _out_proj_ln(xf, ma.reshape(batch * seq, D_SB), mb.reshape(batch * seq, D_CA),
                          w_out[l].astype(_BF16), row(ln2_g[l]), row(ln2_b[l]))
        xf = _ffn_ln(xf, _ff_tiles_in(ffn2_w_gate[l]), _ff_tiles_in(ffn2_w_up[l]), _ff_tiles_out(ffn2_w_down[l]),
                     row(ln3_g[l]), row(ln3_b[l]))
    return xf.reshape(batch, seq, D_MODEL)
```

```python
import math

import numpy as np
import jax
import jax.numpy as jnp
from jax import lax
from jax.experimental import pallas as pl
from jax.experimental.pallas import tpu as pltpu

D_MODEL = 1024
DEPTH = 4
HEAD_DIM = 64
D_SB = 512
D_CA = 512
CHUNK = 64
N_PREV_CHUNKS = 8
MAX_REL = 128
D_FF = 2816
FFN_RESIDUAL = 0.5
ALPHA = (2 * DEPTH) ** 0.25
LN_EPS = 1e-5
RMS_EPS = 1e-6
ATTN_SCALE = 1.0 / math.sqrt(HEAD_DIM)

LANES = 128
HEADS_PER_BLOCK = LANES // HEAD_DIM
VMEM_LIMIT_BYTES = 56 * 1024 * 1024

ROW_TILE = 512
FFN_SUBTILES = 2
FF_TILE = 256
N_FF_TILES = D_FF // FF_TILE
SB_QBLOCK = 64
SB_KBLOCK = LANES
SB_WINDOW = 2 * SB_KBLOCK
SB_PAD = SB_WINDOW - SB_QBLOCK
SB_GROUP = 32
CA_GROUP = 16
CA_PAIR = 2 * CHUNK
CA_PAD = N_PREV_CHUNKS * CHUNK
CA_WINDOW = CA_PAD + CA_PAIR
SB_SKIP_LOG = 88.0
MASK_BIAS = -1e30

_F32 = jnp.float32
_BF16 = jnp.bfloat16


def _layer_norm(y, g, b):
    mu = jnp.mean(y, axis=-1, keepdims=True)
    yc = y - mu
    var = jnp.mean(yc * yc, axis=-1, keepdims=True)
    return yc * lax.rsqrt(var + LN_EPS) * g + b


def _dot(a, b):
    return jnp.dot(a, b, preferred_element_type=_F32)


def _dot_nt(a, b):
    return lax.dot_general(a, b, (((1,), (1,)), ((), ())), preferred_element_type=_F32)


def _split_bf16(a):
    hi = lax.bitcast_convert_type(lax.bitcast_convert_type(a, jnp.uint32) & jnp.uint32(0xFFFF0000), _F32)
    return jnp.concatenate([hi.astype(_BF16), (a - hi).astype(_BF16)], axis=1)


def _softplus(z):
    neg_abs = lax.bitcast_convert_type(lax.bitcast_convert_type(z, jnp.uint32) | jnp.uint32(0x80000000), _F32)
    return jnp.maximum(z, 0.0) + jnp.log(1.0 + jnp.exp(neg_abs))


def _head_rms_gain(o, head0, gain):
    ss = o * o
    s0 = jnp.sum(jnp.where(head0, ss, 0.0), axis=1, keepdims=True)
    s1 = jnp.sum(jnp.where(head0, 0.0, ss), axis=1, keepdims=True)
    r = jnp.where(head0, lax.rsqrt(s0 / HEAD_DIM + RMS_EPS), lax.rsqrt(s1 / HEAD_DIM + RMS_EPS))
    return o * r * gain


def _ffn_ln_kernel(x_ref, wg_ref, wu_ref, wd_ref, g_ref, b_ref, o_ref, xb_ref, acc_ref):
    for t in range(FFN_SUBTILES):
        rows = slice(t * ROW_TILE, (t + 1) * ROW_TILE)
        xb_ref[rows, :] = x_ref[rows, :].astype(_BF16)
        for f in range(N_FF_TILES):
            cols = slice(f * FF_TILE, (f + 1) * FF_TILE)
            xb = xb_ref[rows, :]
            gate = _dot(xb, wg_ref[:, cols])
            up = _dot(xb, wu_ref[:, cols])
            hidden = (gate * jax.nn.sigmoid(gate) * up).astype(_BF16)
            part = _dot(hidden, wd_ref[cols, :])
            if f == 0:
                acc_ref[rows, :] = part
            else:
                acc_ref[rows, :] += part
        y = ALPHA * x_ref[rows, :] + acc_ref[rows, :]
        o_ref[rows, :] = _layer_norm(y, g_ref[...], b_ref[...])


def _ffn_ln(x, wg, wu, wd, g, b):
    rows = x.shape[0]
    step_rows = FFN_SUBTILES * ROW_TILE
    assert rows % step_rows == 0
    const2 = lambda i: (0, 0)
    resident = dict(pipeline_mode=pl.Buffered(1))
    return pl.pallas_call(
        _ffn_ln_kernel,
        grid=(rows // step_rows,),
        in_specs=[
            pl.BlockSpec((step_rows, D_MODEL), lambda i: (i, 0)),
            pl.BlockSpec((D_MODEL, D_FF), const2, **resident),
            pl.BlockSpec((D_MODEL, D_FF), const2, **resident),
            pl.BlockSpec((D_FF, D_MODEL), const2, **resident),
            pl.BlockSpec((1, D_MODEL), const2),
            pl.BlockSpec((1, D_MODEL), const2),
        ],
        out_specs=pl.BlockSpec((step_rows, D_MODEL), lambda i: (i, 0)),
        out_shape=jax.ShapeDtypeStruct((rows, D_MODEL), _F32),
        scratch_shapes=[
            pltpu.VMEM((step_rows, D_MODEL), _BF16),
            pltpu.VMEM((step_rows, D_MODEL), _F32),
        ],
        compiler_params=pltpu.CompilerParams(
            dimension_semantics=("arbitrary",), vmem_limit_bytes=VMEM_LIMIT_BYTES),
        name="ffn_ln",
    )(x, wg, wu, wd, g, b)


IN_PROJ_COLS = 3 * (D_SB + D_CA)
IN_PROJ_TILE = 512


def _in_proj_kernel(x_ref, w_ref, o_ref):
    xb = x_ref[...].astype(_BF16)
    for c in range(IN_PROJ_COLS // IN_PROJ_TILE):
        cols = slice(c * IN_PROJ_TILE, (c + 1) * IN_PROJ_TILE)
        o_ref[:, cols] = _dot(xb, w_ref[:, cols]).astype(_BF16)


def _in_proj(x, w):
    rows = x.shape[0]
    return pl.pallas_call(
        _in_proj_kernel,
        grid=(rows // ROW_TILE,),
        in_specs=[
            pl.BlockSpec((ROW_TILE, D_MODEL), lambda i: (i, 0)),
            pl.BlockSpec((D_MODEL, IN_PROJ_COLS), lambda i: (0, 0)),
        ],
        out_specs=pl.BlockSpec((ROW_TILE, IN_PROJ_COLS), lambda i: (i, 0)),
        out_shape=jax.ShapeDtypeStruct((rows, IN_PROJ_COLS), _BF16),
        compiler_params=pltpu.CompilerParams(
            dimension_semantics=("arbitrary",), vmem_limit_bytes=VMEM_LIMIT_BYTES),
        name="in_proj",
    )(x, w)


def _sb_cumsum_rhs():
    j = np.arange(SB_KBLOCK)[:, None]
    s = np.arange(SB_KBLOCK)[None, :]
    u = np.concatenate([(j >= s), np.ones((SB_KBLOCK, LANES), bool)], axis=1).astype(np.float32)
    return jnp.asarray(np.concatenate([u, u], axis=0), dtype=_BF16)


def _sb_kernel(q_ref, k_ref, v_ref, uu_ref, gain_ref, o_ref, kpad_ref, vpad_ref, acc_ref, spent_ref, flag_ref):
    seq = q_ref.shape[0]
    n_qblocks = seq // SB_QBLOCK
    kpad_ref[0:SB_PAD, :] = jnp.zeros((SB_PAD, LANES), _BF16)
    vpad_ref[0:SB_PAD, :] = jnp.zeros((SB_PAD, LANES), _BF16)
    kpad_ref[SB_PAD:, :] = k_ref[...]
    vpad_ref[SB_PAD:, :] = v_ref[...]

    rows2 = 2 * SB_QBLOCK
    row = lax.broadcasted_iota(jnp.int32, (rows2, LANES), 0)
    lane = lax.broadcasted_iota(jnp.int32, (rows2, LANES), 1)
    own_scale = jnp.where((row < SB_QBLOCK) == (lane < HEAD_DIM), ATTN_SCALE, 0.0).astype(_BF16)
    visible = lane - (SB_KBLOCK - SB_QBLOCK) < (row & (SB_QBLOCK - 1))
    head0 = lax.broadcasted_iota(jnp.int32, (SB_QBLOCK, LANES), 1) < HEAD_DIM
    gain = gain_ref[...]

    def windows(ms):
        r0s = [pl.multiple_of(m * SB_QBLOCK, SB_QBLOCK) for m in ms]
        q2s, zs, sps = [], [], []
        for r0 in r0s:
            q = q_ref[pl.ds(r0, SB_QBLOCK), :]
            q2s.append(jnp.concatenate([q, q], axis=0) * own_scale)
        for r0, q2 in zip(r0s, q2s):
            zs.append(_dot_nt(q2, kpad_ref[pl.ds(r0, SB_WINDOW), :]))
        zs = [(z[:, :SB_KBLOCK], jnp.where(visible, z[:, SB_KBLOCK:], MASK_BIAS)) for z in zs]
        for z_far, z_near in zs:
            sps.append(_softplus(z_far))
            sps.append(_softplus(z_near))
        cs = _dot(_split_bf16(jnp.concatenate(sps, axis=0)), uu_ref[...])
        ws, spents = [], []
        for g, (z_far, z_near) in enumerate(zs):
            far = cs[(2 * g) * rows2:(2 * g + 1) * rows2]
            near = cs[(2 * g + 1) * rows2:(2 * g + 2) * rows2]
            tot_near = near[:, SB_KBLOCK:]
            w_far = jnp.exp(z_far - (far[:, :SB_KBLOCK] + tot_near))
            w_near = jnp.exp(z_near - near[:, :SB_KBLOCK])
            ws.append(jnp.concatenate([w_far, w_near], axis=1).astype(_BF16))
            spents.append(far[:, SB_KBLOCK:] + tot_near)
        accs = [_dot(w, vpad_ref[pl.ds(r0, SB_WINDOW), :]) for r0, w in zip(r0s, ws)]
        return r0s, accs, spents

    def finish(r0, acc):
        o = jnp.where(head0, acc[:SB_QBLOCK], acc[SB_QBLOCK:])
        o_ref[pl.ds(r0, SB_QBLOCK), :] = _head_rms_gain(o, head0, gain).astype(_BF16)

    def first_pass(it, carry):
        ms = [it * SB_GROUP + g for g in range(SB_GROUP)]
        r0s, accs, spents = windows(ms)
        for m, r0, acc, spent in zip(ms, r0s, accs, spents):
            finish(r0, acc)
            a0 = pl.multiple_of(m * rows2, rows2)
            acc_ref[pl.ds(a0, rows2), :] = acc
            spent_ref[pl.ds(a0, rows2), :] = spent
            flag_ref[m] = jnp.min(spent)
        return carry

    lax.fori_loop(0, n_qblocks // SB_GROUP, first_pass, 0)

    def second_pass(m, carry):
        @pl.when(flag_ref[m] <= SB_SKIP_LOG)
        def _():
            r0 = pl.multiple_of(m * SB_QBLOCK, SB_QBLOCK)
            a0 = pl.multiple_of(m * rows2, rows2)
            q = q_ref[pl.ds(r0, SB_QBLOCK), :]
            q2 = jnp.concatenate([q, q], axis=0) * own_scale

            def more(state):
                j, _, spent = state
                return jnp.logical_and(r0 - j * SB_KBLOCK > SB_PAD, jnp.min(spent) <= SB_SKIP_LOG)

            def key_block(state):
                j, acc, spent = state
                rr = pl.multiple_of(r0 - (j + 1) * SB_KBLOCK, SB_QBLOCK)
                kb = kpad_ref[pl.ds(rr, SB_KBLOCK), :]
                vb = vpad_ref[pl.ds(rr, SB_KBLOCK), :]
                z = _dot_nt(q2, kb)
                cs = _dot(_split_bf16(_softplus(z)), uu_ref[...])
                w = jnp.exp(z - (cs[:, :SB_KBLOCK] + spent))
                return j + 1, acc + _dot(w.astype(_BF16), vb), spent + cs[:, SB_KBLOCK:]

            start = (0, acc_ref[pl.ds(a0, rows2), :], spent_ref[pl.ds(a0, rows2), :])
            _, acc, _ = lax.while_loop(more, key_block, start)
            finish(r0, acc)
        return carry

    lax.fori_loop(0, n_qblocks, second_pass, 0)


def _sb_attn(h, gain):
    batch, seq, _ = h.shape
    n_pairs = D_SB // LANES
    col_blocks = D_SB // LANES
    uu = _sb_cumsum_rhs()
    return pl.pallas_call(
        _sb_kernel,
        grid=(batch, n_pairs),
        in_specs=[
            pl.BlockSpec((None, seq, LANES), lambda b, p: (b, 0, p)),
            pl.BlockSpec((None, seq, LANES), lambda b, p: (b, 0, col_blocks + p)),
            pl.BlockSpec((None, seq, LANES), lambda b, p: (b, 0, 2 * col_blocks + p)),
            pl.BlockSpec(uu.shape, lambda b, p: (0, 0)),
            pl.BlockSpec((1, LANES), lambda b, p: (0, p)),
        ],
        out_specs=pl.BlockSpec((None, seq, LANES), lambda b, p: (b, 0, p)),
        out_shape=jax.ShapeDtypeStruct((batch, seq, D_SB), _BF16),
        scratch_shapes=[
            pltpu.VMEM((seq + SB_PAD, LANES), _BF16),
            pltpu.VMEM((seq + SB_PAD, LANES), _BF16),
            pltpu.VMEM((HEADS_PER_BLOCK * seq, LANES), _F32),
            pltpu.VMEM((HEADS_PER_BLOCK * seq, LANES), _F32),
            pltpu.SMEM((seq // SB_QBLOCK,), _F32),
        ],
        compiler_params=pltpu.CompilerParams(
            dimension_semantics=("arbitrary", "arbitrary"), vmem_limit_bytes=VMEM_LIMIT_BYTES),
        name="sb_attn",
    )(h, h, h, uu, gain)


def _ca_bias_table(rel_bias):
    n_heads = rel_bias.shape[0]
    row_len = CA_WINDOW + CA_PAIR
    period = row_len + 1
    n_far = CA_PAD - MAX_REL + 1
    far = rel_bias[:, 2 * MAX_REL:]
    ramp = rel_bias[:, 1:2 * MAX_REL][:, ::-1]
    by_offset = jnp.concatenate([
        jnp.broadcast_to(far, (n_heads, n_far)), ramp,
        jnp.broadcast_to(far, (n_heads, period - n_far - ramp.shape[1]))], axis=1)
    table = jnp.tile(by_offset, (1, CA_PAIR))[:, :CA_PAIR * row_len]
    table = table.reshape(n_heads, CA_PAIR, row_len)[:, :, :CA_WINDOW]
    qi = np.arange(CA_PAIR)[:, None]
    kj = np.arange(CA_WINDOW)[None, :]
    visible = np.where(qi < CHUNK, kj < CA_PAD + CHUNK, kj >= CHUNK)
    table = jnp.where(jnp.asarray(visible)[None], table.astype(_F32), MASK_BIAS)
    return table.reshape(-1, CA_WINDOW)


def _ca_kernel(q_ref, k_ref, v_ref, bias_ref, gain_ref, o_ref, kpad_ref, vpad_ref):
    seq = q_ref.shape[0]
    n_pairs = seq // CA_PAIR
    kpad_ref[0:CA_PAD, :] = jnp.zeros((CA_PAD, LANES), _BF16)
    vpad_ref[0:CA_PAD, :] = jnp.zeros((CA_PAD, LANES), _BF16)
    kpad_ref[CA_PAD:, :] = k_ref[...]
    vpad_ref[CA_PAD:, :] = v_ref[...]

    rows2 = 2 * CA_PAIR
    row = lax.broadcasted_iota(jnp.int32, (rows2, LANES), 0)
    lane = lax.broadcasted_iota(jnp.int32, (rows2, LANES), 1)
    own_scale = jnp.where((row < CA_PAIR) == (lane < HEAD_DIM), ATTN_SCALE, 0.0).astype(_BF16)
    head0 = lax.broadcasted_iota(jnp.int32, (CA_PAIR, LANES), 1) < HEAD_DIM
    gain = gain_ref[...]

    def pairs(c2s, first_group):
        if first_group:
            r0s = [c2 * CA_PAIR for c2 in c2s]
        else:
            r0s = [pl.multiple_of(c2 * CA_PAIR, CA_PAIR) for c2 in c2s]
        q2s, scs, ps, denoms = [], [], [], []
        for r0 in r0s:
            q = q_ref[pl.ds(r0, CA_PAIR), :]
            q2s.append(jnp.concatenate([q, q], axis=0) * own_scale)
        for r0, q2 in zip(r0s, q2s):
            sc = _dot_nt(q2, kpad_ref[pl.ds(r0, CA_WINDOW), :]) + bias_ref[...]
            if first_group and r0 < CA_PAD:
                col = lax.broadcasted_iota(jnp.int32, (rows2, CA_WINDOW), 1)
                sc = jnp.where(col >= CA_PAD - r0, sc, MASK_BIAS)
            scs.append(sc)
        for sc in scs:
            p = jnp.exp(sc - jnp.max(sc, axis=1, keepdims=True))
            denoms.append(jnp.sum(p, axis=1, keepdims=True))
            ps.append(p.astype(_BF16))
        accs = [_dot(p, vpad_ref[pl.ds(r0, CA_WINDOW), :]) for r0, p in zip(r0s, ps)]
        for r0, acc, denom in zip(r0s, accs, denoms):
            acc = acc / denom
            o = jnp.where(head0, acc[:CA_PAIR], acc[CA_PAIR:])
            o_ref[pl.ds(r0, CA_PAIR), :] = _head_rms_gain(o, head0, gain).astype(_BF16)

    pairs(list(range(CA_GROUP)), True)

    def later_pairs(it, carry):
        pairs([it * CA_GROUP + g for g in range(CA_GROUP)], False)
        return carry

    lax.fori_loop(1, n_pairs // CA_GROUP, later_pairs, 0)


def _ca_attn(h, bias, gain):
    batch, seq, _ = h.shape
    n_pairs = D_CA // LANES
    first = 3 * D_SB // LANES
    col_blocks = D_CA // LANES
    return pl.pallas_call(
        _ca_kernel,
        grid=(batch, n_pairs),
        in_specs=[
            pl.BlockSpec((None, seq, LANES), lambda b, p: (b, 0, first + p)),
            pl.BlockSpec((None, seq, LANES), lambda b, p: (b, 0, first + col_blocks + p)),
            pl.BlockSpec((None, seq, LANES), lambda b, p: (b, 0, first + 2 * col_blocks + p)),
            pl.BlockSpec((HEADS_PER_BLOCK * CA_PAIR, CA_WINDOW), lambda b, p: (p, 0)),
            pl.BlockSpec((1, LANES), lambda b, p: (0, p)),
        ],
        out_specs=pl.BlockSpec((None, seq, LANES), lambda b, p: (b, 0, p)),
        out_shape=jax.ShapeDtypeStruct((batch, seq, D_CA), _BF16),
        scratch_shapes=[
            pltpu.VMEM((seq + CA_PAD, LANES), _BF16),
            pltpu.VMEM((seq + CA_PAD, LANES), _BF16),
        ],
        compiler_params=pltpu.CompilerParams(
            dimension_semantics=("arbitrary", "arbitrary"), vmem_limit_bytes=VMEM_LIMIT_BYTES),
        name="ca_attn",
    )(h, h, h, bias, gain)


def _out_proj_ln_kernel(x_ref, ma_ref, mb_ref, w_ref, g_ref, b_ref, o_ref):
    y = _dot(ma_ref[...], w_ref[0:D_SB, :]) + _dot(mb_ref[...], w_ref[D_SB:, :])
    o_ref[...] = _layer_norm(ALPHA * x_ref[...] + y, g_ref[...], b_ref[...])


def _out_proj_ln(x, ma, mb, w, g, b):
    rows = x.shape[0]
    const2 = lambda i: (0, 0)
    return pl.pallas_call(
        _out_proj_ln_kernel,
        grid=(rows // ROW_TILE,),
        in_specs=[
            pl.BlockSpec((ROW_TILE, D_MODEL), lambda i: (i, 0)),
            pl.BlockSpec((ROW_TILE, D_SB), lambda i: (i, 0)),
            pl.BlockSpec((ROW_TILE, D_CA), lambda i: (i, 0)),
            pl.BlockSpec((D_SB + D_CA, D_MODEL), const2),
            pl.BlockSpec((1, D_MODEL), const2),
            pl.BlockSpec((1, D_MODEL), const2),
        ],
        out_specs=pl.BlockSpec((ROW_TILE, D_MODEL), lambda i: (i, 0)),
        out_shape=jax.ShapeDtypeStruct((rows, D_MODEL), _F32),
        compiler_params=pltpu.CompilerParams(
            dimension_semantics=("arbitrary",), vmem_limit_bytes=VMEM_LIMIT_BYTES),
        name="out_proj_ln",
    )(x, ma, mb, w, g, b)


def _half_bf16(w):
    return (FFN_RESIDUAL * w).astype(_BF16)


def kernel(x, ffn1_w_gate, ffn1_w_up, ffn1_w_down, ln1_g, ln1_b, w_in, rel_bias, sb_out_g, ca_out_g, w_out,
           ln2_g, ln2_b, ffn2_w_gate, ffn2_w_up, ffn2_w_down, ln3_g, ln3_b):
    batch, seq, _ = x.shape
    assert CA_GROUP * CA_PAIR >= CA_PAD and seq % (CA_PAIR * CA_GROUP) == 0 and seq % (SB_QBLOCK * SB_GROUP) == 0 and (batch * seq) % (FFN_SUBTILES * ROW_TILE) == 0
    row = lambda a: a.reshape(1, -1)
    xf = x.reshape(batch * seq, D_MODEL)
    for l in range(DEPTH):
        xf = _ffn_ln(xf, ffn1_w_gate[l].astype(_BF16), ffn1_w_up[l].astype(_BF16), _half_bf16(ffn1_w_down[l]),
                     row(ln1_g[l]), row(ln1_b[l]))
        h = _in_proj(xf, w_in[l].astype(_BF16)).reshape(batch, seq, IN_PROJ_COLS)
        ma = _sb_attn(h, row(sb_out_g[l]))
        mb = _ca_attn(h, _ca_bias_table(rel_bias[l]), row(ca_out_g[l]))
        xf = _out_proj_ln(xf, ma.reshape(batch * seq, D_SB), mb.reshape(batch * seq, D_CA),
                          w_out[l].astype(_BF16), row(ln2_g[l]), row(ln2_b[l]))
        xf = _ffn_ln(xf, ffn2_w_gate[l].astype(_BF16), ffn2_w_up[l].astype(_BF16), _half_bf16(ffn2_w_down[l]),
                     row(ln3_g[l]), row(ln3_b[l]))
    return xf.reshape(batch, seq, D_MODEL)
```

```python
import math

import numpy as np
import jax
import jax.numpy as jnp
from jax import lax
from jax.experimental import pallas as pl
from jax.experimental.pallas import tpu as pltpu

D_MODEL = 1024
DEPTH = 4
HEAD_DIM = 64
D_SB = 512
D_CA = 512
CHUNK = 64
N_PREV_CHUNKS = 8
MAX_REL = 128
D_FF = 2816
FFN_RESIDUAL = 0.5
ALPHA = (2 * DEPTH) ** 0.25
LN_EPS = 1e-5
RMS_EPS = 1e-6
ATTN_SCALE = 1.0 / math.sqrt(HEAD_DIM)

LANES = 128
HEADS_PER_BLOCK = LANES // HEAD_DIM
VMEM_LIMIT_BYTES = 56 * 1024 * 1024

ROW_TILE = 512
FFN_SUBTILES = 2
FF_TILE = 256
N_FF_TILES = D_FF // FF_TILE
SB_QBLOCK = 64
SB_KBLOCK = LANES
SB_WINDOW = 2 * SB_KBLOCK
SB_PAD = SB_WINDOW - SB_QBLOCK
SB_GROUP = 32
CA_GROUP = 16
CA_PAIR = 2 * CHUNK
CA_PAD = N_PREV_CHUNKS * CHUNK
CA_WINDOW = CA_PAD + CA_PAIR
SB_SKIP_LOG = 88.0
MASK_BIAS = -1e30

_F32 = jnp.float32
_BF16 = jnp.bfloat16


def _layer_norm(y, g, b):
    mu = jnp.mean(y, axis=-1, keepdims=True)
    yc = y - mu
    var = jnp.mean(yc * yc, axis=-1, keepdims=True)
    return yc * lax.rsqrt(var + LN_EPS) * g + b


def _dot(a, b):
    return jnp.dot(a, b, preferred_element_type=_F32)


def _dot_nt(a, b):
    return lax.dot_general(a, b, (((1,), (1,)), ((), ())), preferred_element_type=_F32)


def _split_bf16(a):
    hi = lax.bitcast_convert_type(lax.bitcast_convert_type(a, jnp.uint32) & jnp.uint32(0xFFFF0000), _F32)
    return jnp.concatenate([hi.astype(_BF16), (a - hi).astype(_BF16)], axis=1)


def _softplus(z):
    neg_abs = lax.bitcast_convert_type(lax.bitcast_convert_type(z, jnp.uint32) | jnp.uint32(0x80000000), _F32)
    return jnp.maximum(z, 0.0) + jnp.log(1.0 + jnp.exp(neg_abs))


def _head_rms_gain(o, head0, gain):
    ss = o * o
    s0 = jnp.sum(jnp.where(head0, ss, 0.0), axis=1, keepdims=True)
    s1 = jnp.sum(jnp.where(head0, 0.0, ss), axis=1, keepdims=True)
    r = jnp.where(head0, lax.rsqrt(s0 / HEAD_DIM + RMS_EPS), lax.rsqrt(s1 / HEAD_DIM + RMS_EPS))
    return o * r * gain


def _swiglu_ln_rows(rows, x, xb_ref, acc_ref, wg_ref, wu_ref, wd_ref, g_ref, b_ref):
    xb_ref[rows, :] = x.astype(_BF16)
    for f in range(N_FF_TILES):
        cols = slice(f * FF_TILE, (f + 1) * FF_TILE)
        xb = xb_ref[rows, :]
        gate = _dot(xb, wg_ref[:, cols])
        up = _dot(xb, wu_ref[:, cols])
        hidden = (gate * jax.nn.sigmoid(gate) * up).astype(_BF16)
        part = _dot(hidden, wd_ref[cols, :])
        if f == 0:
            acc_ref[rows, :] = part
        else:
            acc_ref[rows, :] += part
    return _layer_norm(ALPHA * x + acc_ref[rows, :], g_ref[...], b_ref[...])


def _ffn_ln_kernel(x_ref, wg_ref, wu_ref, wd_ref, g_ref, b_ref, o_ref, xb_ref, acc_ref):
    for t in range(FFN_SUBTILES):
        rows = slice(t * ROW_TILE, (t + 1) * ROW_TILE)
        o_ref[rows, :] = _swiglu_ln_rows(rows, x_ref[rows, :], xb_ref, acc_ref,
                                         wg_ref, wu_ref, wd_ref, g_ref, b_ref)


def _mix_ffn_ln_kernel(x_ref, ma_ref, mb_ref, wo_ref, g2_ref, b2_ref, wg_ref, wu_ref, wd_ref, g3_ref, b3_ref,
                       o_ref, x2_ref, xb_ref, acc_ref):
    for t in range(FFN_SUBTILES):
        rows = slice(t * ROW_TILE, (t + 1) * ROW_TILE)
        slabs = [ma_ref[p, rows, :] for p in range(D_SB // LANES)] + [mb_ref[p, rows, :] for p in range(D_CA // LANES)]
        y = ALPHA * x_ref[rows, :] + _dot(jnp.concatenate(slabs, axis=1), wo_ref[...])
        x2_ref[rows, :] = _layer_norm(y, g2_ref[...], b2_ref[...])
        o_ref[rows, :] = _swiglu_ln_rows(rows, x2_ref[rows, :], xb_ref, acc_ref,
                                         wg_ref, wu_ref, wd_ref, g3_ref, b3_ref)


FFN_STEP_ROWS = FFN_SUBTILES * ROW_TILE
_CONST2 = lambda i: (0, 0)
_RESIDENT = dict(pipeline_mode=pl.Buffered(1))
_ROWS_SPEC = pl.BlockSpec((FFN_STEP_ROWS, D_MODEL), lambda i: (i, 0))
_FFN_WEIGHT_SPECS = [
    pl.BlockSpec((D_MODEL, D_FF), _CONST2, **_RESIDENT),
    pl.BlockSpec((D_MODEL, D_FF), _CONST2, **_RESIDENT),
    pl.BlockSpec((D_FF, D_MODEL), _CONST2, **_RESIDENT),
    pl.BlockSpec((1, D_MODEL), _CONST2),
    pl.BlockSpec((1, D_MODEL), _CONST2),
]
_FFN_SCRATCH = [pltpu.VMEM((FFN_STEP_ROWS, D_MODEL), _BF16), pltpu.VMEM((FFN_STEP_ROWS, D_MODEL), _F32)]
_ROWWISE_PARAMS = pltpu.CompilerParams(dimension_semantics=("arbitrary",), vmem_limit_bytes=VMEM_LIMIT_BYTES)


def _ffn_ln(x, wg, wu, wd, g, b):
    rows = x.shape[0]
    assert rows % FFN_STEP_ROWS == 0
    return pl.pallas_call(
        _ffn_ln_kernel,
        grid=(rows // FFN_STEP_ROWS,),
        in_specs=[_ROWS_SPEC] + _FFN_WEIGHT_SPECS,
        out_specs=_ROWS_SPEC,
        out_shape=jax.ShapeDtypeStruct((rows, D_MODEL), _F32),
        scratch_shapes=_FFN_SCRATCH,
        compiler_params=_ROWWISE_PARAMS,
        name="ffn_ln",
    )(x, wg, wu, wd, g, b)


def _mix_ffn_ln(x, ma, mb, wo, g2, b2, wg, wu, wd, g3, b3):
    rows = x.shape[0]
    assert rows % FFN_STEP_ROWS == 0
    slab_spec = lambda n: pl.BlockSpec((n // LANES, FFN_STEP_ROWS, LANES), lambda i: (0, i, 0))
    return pl.pallas_call(
        _mix_ffn_ln_kernel,
        grid=(rows // FFN_STEP_ROWS,),
        in_specs=[_ROWS_SPEC, slab_spec(D_SB), slab_spec(D_CA),
                  pl.BlockSpec((D_SB + D_CA, D_MODEL), _CONST2, **_RESIDENT),
                  pl.BlockSpec((1, D_MODEL), _CONST2), pl.BlockSpec((1, D_MODEL), _CONST2)] + _FFN_WEIGHT_SPECS,
        out_specs=_ROWS_SPEC,
        out_shape=jax.ShapeDtypeStruct((rows, D_MODEL), _F32),
        scratch_shapes=[pltpu.VMEM((FFN_STEP_ROWS, D_MODEL), _F32)] + _FFN_SCRATCH,
        compiler_params=_ROWWISE_PARAMS,
        name="mix_ffn_ln",
    )(x, ma, mb, wo, g2, b2, wg, wu, wd, g3, b3)


IN_PROJ_COLS = 3 * (D_SB + D_CA)
IN_PROJ_TILE = 512


def _in_proj_kernel(x_ref, w_ref, o_ref):
    xb = x_ref[...].astype(_BF16)
    slabs_per_tile = IN_PROJ_TILE // LANES
    for c in range(IN_PROJ_COLS // IN_PROJ_TILE):
        cols = slice(c * IN_PROJ_TILE, (c + 1) * IN_PROJ_TILE)
        res = _dot(xb, w_ref[:, cols]).astype(_BF16)
        for j in range(slabs_per_tile):
            o_ref[c * slabs_per_tile + j] = res[:, j * LANES:(j + 1) * LANES]


def _in_proj(x, w):
    rows = x.shape[0]
    n_slabs = IN_PROJ_COLS // LANES
    return pl.pallas_call(
        _in_proj_kernel,
        grid=(rows // ROW_TILE,),
        in_specs=[
            pl.BlockSpec((ROW_TILE, D_MODEL), lambda i: (i, 0)),
            pl.BlockSpec((D_MODEL, IN_PROJ_COLS), lambda i: (0, 0)),
        ],
        out_specs=pl.BlockSpec((n_slabs, ROW_TILE, LANES), lambda i: (0, i, 0)),
        out_shape=jax.ShapeDtypeStruct((n_slabs, rows, LANES), _BF16),
        compiler_params=_ROWWISE_PARAMS,
        name="in_proj",
    )(x, w)


def _sb_cumsum_rhs():
    j = np.arange(SB_KBLOCK)[:, None]
    s = np.arange(SB_KBLOCK)[None, :]
    u = np.concatenate([(j >= s), np.ones((SB_KBLOCK, LANES), bool)], axis=1).astype(np.float32)
    return jnp.asarray(np.concatenate([u, u], axis=0), dtype=_BF16)


def _sb_kernel(q_ref, k_ref, v_ref, uu_ref, gain_ref, o_ref, kpad_ref, vpad_ref, acc_ref, spent_ref, flag_ref):
    seq = q_ref.shape[0]
    n_qblocks = seq // SB_QBLOCK
    kpad_ref[0:SB_PAD, :] = jnp.zeros((SB_PAD, LANES), _BF16)
    vpad_ref[0:SB_PAD, :] = jnp.zeros((SB_PAD, LANES), _BF16)
    kpad_ref[SB_PAD:, :] = k_ref[...]
    vpad_ref[SB_PAD:, :] = v_ref[...]

    rows2 = 2 * SB_QBLOCK
    row = lax.broadcasted_iota(jnp.int32, (rows2, LANES), 0)
    lane = lax.broadcasted_iota(jnp.int32, (rows2, LANES), 1)
    own_scale = jnp.where((row < SB_QBLOCK) == (lane < HEAD_DIM), ATTN_SCALE, 0.0).astype(_BF16)
    visible = lane - (SB_KBLOCK - SB_QBLOCK) < (row & (SB_QBLOCK - 1))
    head0 = lax.broadcasted_iota(jnp.int32, (SB_QBLOCK, LANES), 1) < HEAD_DIM
    gain = gain_ref[...]

    def windows(ms):
        r0s = [pl.multiple_of(m * SB_QBLOCK, SB_QBLOCK) for m in ms]
        q2s, zs, sps = [], [], []
        for r0 in r0s:
            q = q_ref[pl.ds(r0, SB_QBLOCK), :]
            q2s.append(jnp.concatenate([q, q], axis=0) * own_scale)
        for r0, q2 in zip(r0s, q2s):
            zs.append(_dot_nt(q2, kpad_ref[pl.ds(r0, SB_WINDOW), :]))
        zs = [(z[:, :SB_KBLOCK], jnp.where(visible, z[:, SB_KBLOCK:], MASK_BIAS)) for z in zs]
        for z_far, z_near in zs:
            sps.append(_softplus(z_far))
            sps.append(_softplus(z_near))
        cs = _dot(_split_bf16(jnp.concatenate(sps, axis=0)), uu_ref[...])
        ws, spents = [], []
        for g, (z_far, z_near) in enumerate(zs):
            far = cs[(2 * g) * rows2:(2 * g + 1) * rows2]
            near = cs[(2 * g + 1) * rows2:(2 * g + 2) * rows2]
            tot_near = near[:, SB_KBLOCK:]
            w_far = jnp.exp(z_far - (far[:, :SB_KBLOCK] + tot_near))
            w_near = jnp.exp(z_near - near[:, :SB_KBLOCK])
            ws.append(jnp.concatenate([w_far, w_near], axis=1).astype(_BF16))
            spents.append(far[:, SB_KBLOCK:] + tot_near)
        accs = [_dot(w, vpad_ref[pl.ds(r0, SB_WINDOW), :]) for r0, w in zip(r0s, ws)]
        return r0s, accs, spents

    def finish(r0, acc):
        o = jnp.where(head0, acc[:SB_QBLOCK], acc[SB_QBLOCK:])
        o_ref[pl.ds(r0, SB_QBLOCK), :] = _head_rms_gain(o, head0, gain).astype(_BF16)

    def first_pass(it, carry):
        ms = [it * SB_GROUP + g for g in range(SB_GROUP)]
        r0s, accs, spents = windows(ms)
        for m, r0, acc, spent in zip(ms, r0s, accs, spents):
            finish(r0, acc)
            a0 = pl.multiple_of(m * rows2, rows2)
            acc_ref[pl.ds(a0, rows2), :] = acc
            spent_ref[pl.ds(a0, rows2), :] = spent
            flag_ref[m] = jnp.min(spent)
        return carry

    lax.fori_loop(0, n_qblocks // SB_GROUP, first_pass, 0)

    def second_pass(m, carry):
        @pl.when(flag_ref[m] <= SB_SKIP_LOG)
        def _():
            r0 = pl.multiple_of(m * SB_QBLOCK, SB_QBLOCK)
            a0 = pl.multiple_of(m * rows2, rows2)
            q = q_ref[pl.ds(r0, SB_QBLOCK), :]
            q2 = jnp.concatenate([q, q], axis=0) * own_scale

            def more(state):
                j, _, spent = state
                return jnp.logical_and(r0 - j * SB_KBLOCK > SB_PAD, jnp.min(spent) <= SB_SKIP_LOG)

            def key_block(state):
                j, acc, spent = state
                rr = pl.multiple_of(r0 - (j + 1) * SB_KBLOCK, SB_QBLOCK)
                kb = kpad_ref[pl.ds(rr, SB_KBLOCK), :]
                vb = vpad_ref[pl.ds(rr, SB_KBLOCK), :]
                z = _dot_nt(q2, kb)
                cs = _dot(_split_bf16(_softplus(z)), uu_ref[...])
                w = jnp.exp(z - (cs[:, :SB_KBLOCK] + spent))
                return j + 1, acc + _dot(w.astype(_BF16), vb), spent + cs[:, SB_KBLOCK:]

            start = (0, acc_ref[pl.ds(a0, rows2), :], spent_ref[pl.ds(a0, rows2), :])
            _, acc, _ = lax.while_loop(more, key_block, start)
            finish(r0, acc)
        return carry

    lax.fori_loop(0, n_qblocks, second_pass, 0)


def _sb_attn(h, gain, batch):
    rows = h.shape[1]
    seq = rows // batch
    n_pairs = D_SB // LANES
    col_blocks = D_SB // LANES
    uu = _sb_cumsum_rhs()
    return pl.pallas_call(
        _sb_kernel,
        grid=(batch, n_pairs),
        in_specs=[
            pl.BlockSpec((None, seq, LANES), lambda b, p: (p, b, 0)),
            pl.BlockSpec((None, seq, LANES), lambda b, p: (col_blocks + p, b, 0)),
            pl.BlockSpec((None, seq, LANES), lambda b, p: (2 * col_blocks + p, b, 0)),
            pl.BlockSpec(uu.shape, lambda b, p: (0, 0)),
            pl.BlockSpec((1, LANES), lambda b, p: (0, p)),
        ],
        out_specs=pl.BlockSpec((None, seq, LANES), lambda b, p: (p, b, 0)),
        out_shape=jax.ShapeDtypeStruct((n_pairs, rows, LANES), _BF16),
        scratch_shapes=[
            pltpu.VMEM((seq + SB_PAD, LANES), _BF16),
            pltpu.VMEM((seq + SB_PAD, LANES), _BF16),
            pltpu.VMEM((HEADS_PER_BLOCK * seq, LANES), _F32),
            pltpu.VMEM((HEADS_PER_BLOCK * seq, LANES), _F32),
            pltpu.SMEM((seq // SB_QBLOCK,), _F32),
        ],
        compiler_params=pltpu.CompilerParams(
            dimension_semantics=("arbitrary", "arbitrary"), vmem_limit_bytes=VMEM_LIMIT_BYTES),
        name="sb_attn",
    )(h, h, h, uu, gain)


def _ca_bias_table(rel_bias):
    n_heads = rel_bias.shape[0]
    row_len = CA_WINDOW + CA_PAIR
    period = row_len + 1
    n_far = CA_PAD - MAX_REL + 1
    far = rel_bias[:, 2 * MAX_REL:]
    ramp = rel_bias[:, 1:2 * MAX_REL][:, ::-1]
    by_offset = jnp.concatenate([
        jnp.broadcast_to(far, (n_heads, n_far)), ramp,
        jnp.broadcast_to(far, (n_heads, period - n_far - ramp.shape[1]))], axis=1)
    table = jnp.tile(by_offset, (1, CA_PAIR))[:, :CA_PAIR * row_len]
    table = table.reshape(n_heads, CA_PAIR, row_len)[:, :, :CA_WINDOW]
    qi = np.arange(CA_PAIR)[:, None]
    kj = np.arange(CA_WINDOW)[None, :]
    visible = np.where(qi < CHUNK, kj < CA_PAD + CHUNK, kj >= CHUNK)
    table = jnp.where(jnp.asarray(visible)[None], table.astype(_F32), MASK_BIAS)
    return table.reshape(-1, CA_WINDOW)


def _ca_kernel(q_ref, k_ref, v_ref, bias_ref, gain_ref, o_ref, kpad_ref, vpad_ref):
    seq = q_ref.shape[0]
    n_pairs = seq // CA_PAIR
    kpad_ref[0:CA_PAD, :] = jnp.zeros((CA_PAD, LANES), _BF16)
    vpad_ref[0:CA_PAD, :] = jnp.zeros((CA_PAD, LANES), _BF16)
    kpad_ref[CA_PAD:, :] = k_ref[...]
    vpad_ref[CA_PAD:, :] = v_ref[...]

    rows2 = 2 * CA_PAIR
    row = lax.broadcasted_iota(jnp.int32, (rows2, LANES), 0)
    lane = lax.broadcasted_iota(jnp.int32, (rows2, LANES), 1)
    own_scale = jnp.where((row < CA_PAIR) == (lane < HEAD_DIM), ATTN_SCALE, 0.0).astype(_BF16)
    head0 = lax.broadcasted_iota(jnp.int32, (CA_PAIR, LANES), 1) < HEAD_DIM
    gain = gain_ref[...]

    def pairs(c2s, first_group):
        if first_group:
            r0s = [c2 * CA_PAIR for c2 in c2s]
        else:
            r0s = [pl.multiple_of(c2 * CA_PAIR, CA_PAIR) for c2 in c2s]
        q2s, scs, ps, denoms = [], [], [], []
        for r0 in r0s:
            q = q_ref[pl.ds(r0, CA_PAIR), :]
            q2s.append(jnp.concatenate([q, q], axis=0) * own_scale)
        for r0, q2 in zip(r0s, q2s):
            sc = _dot_nt(q2, kpad_ref[pl.ds(r0, CA_WINDOW), :]) + bias_ref[...]
            if first_group and r0 < CA_PAD:
                col = lax.broadcasted_iota(jnp.int32, (rows2, CA_WINDOW), 1)
                sc = jnp.where(col >= CA_PAD - r0, sc, MASK_BIAS)
            scs.append(sc)
        for sc in scs:
            p = jnp.exp(sc - jnp.max(sc, axis=1, keepdims=True))
            denoms.append(jnp.sum(p, axis=1, keepdims=True))
            ps.append(p.astype(_BF16))
        accs = [_dot(p, vpad_ref[pl.ds(r0, CA_WINDOW), :]) for r0, p in zip(r0s, ps)]
        for r0, acc, denom in zip(r0s, accs, denoms):
            acc = acc / denom
            o = jnp.where(head0, acc[:CA_PAIR], acc[CA_PAIR:])
            o_ref[pl.ds(r0, CA_PAIR), :] = _head_rms_gain(o, head0, gain).astype(_BF16)

    pairs(list(range(CA_GROUP)), True)

    def later_pairs(it, carry):
        pairs([it * CA_GROUP + g for g in range(CA_GROUP)], False)
        return carry

    lax.fori_loop(1, n_pairs // CA_GROUP, later_pairs, 0)


def _ca_attn(h, bias, gain, batch):
    rows = h.shape[1]
    seq = rows // batch
    n_pairs = D_CA // LANES
    first = 3 * D_SB // LANES
    col_blocks = D_CA // LANES
    return pl.pallas_call(
        _ca_kernel,
        grid=(batch, n_pairs),
        in_specs=[
            pl.BlockSpec((None, seq, LANES), lambda b, p: (first + p, b, 0)),
            pl.BlockSpec((None, seq, LANES), lambda b, p: (first + col_blocks + p, b, 0)),
            pl.BlockSpec((None, seq, LANES), lambda b, p: (first + 2 * col_blocks + p, b, 0)),
            pl.BlockSpec((HEADS_PER_BLOCK * CA_PAIR, CA_WINDOW), lambda b, p: (p, 0)),
            pl.BlockSpec((1, LANES), lambda b, p: (0, p)),
        ],
        out_specs=pl.BlockSpec((None, seq, LANES), lambda b, p: (p, b, 0)),
        out_shape=jax.ShapeDtypeStruct((n_pairs, rows, LANES), _BF16),
        scratch_shapes=[
            pltpu.VMEM((seq + CA_PAD, LANES), _BF16),
            pltpu.VMEM((seq + CA_PAD, LANES), _BF16),
        ],
        compiler_params=pltpu.CompilerParams(
            dimension_semantics=("arbitrary", "arbitrary"), vmem_limit_bytes=VMEM_LIMIT_BYTES),
        name="ca_attn",
    )(h, h, h, bias, gain)


def _half_bf16(w):
    return (FFN_RESIDUAL * w).astype(_BF16)


def kernel(x, ffn1_w_gate, ffn1_w_up, ffn1_w_down, ln1_g, ln1_b, w_in, rel_bias, sb_out_g, ca_out_g, w_out,
           ln2_g, ln2_b, ffn2_w_gate, ffn2_w_up, ffn2_w_down, ln3_g, ln3_b):
    batch, seq, _ = x.shape
    assert CA_GROUP * CA_PAIR >= CA_PAD and seq % (CA_PAIR * CA_GROUP) == 0 and seq % (SB_QBLOCK * SB_GROUP) == 0 and (batch * seq) % (FFN_SUBTILES * ROW_TILE) == 0
    row = lambda a: a.reshape(1, -1)
    xf = x.reshape(batch * seq, D_MODEL)
    for l in range(DEPTH):
        xf = _ffn_ln(xf, ffn1_w_gate[l].astype(_BF16), ffn1_w_up[l].astype(_BF16), _half_bf16(ffn1_w_down[l]),
                     row(ln1_g[l]), row(ln1_b[l]))
        h = _in_proj(xf, w_in[l].astype(_BF16))
        ma = _sb_attn(h, row(sb_out_g[l]), batch)
        mb = _ca_attn(h, _ca_bias_table(rel_bias[l]), row(ca_out_g[l]), batch)
        xf = _mix_ffn_ln(xf, ma, mb, w_out[l].astype(_BF16), row(ln2_g[l]), row(ln2_b[l]),
                         ffn2_w_gate[l].astype(_BF16), ffn2_w_up[l].astype(_BF16), _half_bf16(ffn2_w_down[l]),
                         row(ln3_g[l]), row(ln3_b[l]))
    return xf.reshape(batch, seq, D_MODEL)
```

```python
import math

import numpy as np
import jax
import jax.numpy as jnp
from jax import lax
from jax.experimental import pallas as pl
from jax.experimental.pallas import tpu as pltpu

D_MODEL = 1024
DEPTH = 4
HEAD_DIM = 64
D_SB = 512
D_CA = 512
CHUNK = 64
N_PREV_CHUNKS = 8
MAX_REL = 128
D_FF = 2816
FFN_RESIDUAL = 0.5
ALPHA = (2 * DEPTH) ** 0.25
LN_EPS = 1e-5
RMS_EPS = 1e-6
ATTN_SCALE = 1.0 / math.sqrt(HEAD_DIM)

LANES = 128
HEADS_PER_BLOCK = LANES // HEAD_DIM
VMEM_LIMIT_BYTES = 56 * 1024 * 1024

ROW_TILE = 512
FFN_SUBTILES = 2
FF_TILE = 256
N_FF_TILES = D_FF // FF_TILE
SB_QBLOCK = 64
SB_KBLOCK = LANES
SB_WINDOW = 2 * SB_KBLOCK
SB_PAD = SB_WINDOW - SB_QBLOCK
SB_GROUP = 32
CA_GROUP = 16
CA_PAIR = 2 * CHUNK
CA_PAD = N_PREV_CHUNKS * CHUNK
CA_WINDOW = CA_PAD + CA_PAIR
SB_SKIP_LOG = 88.0
MASK_BIAS = -1e30

_F32 = jnp.float32
_BF16 = jnp.bfloat16


def _layer_norm(y, g, b):
    mu = jnp.mean(y, axis=-1, keepdims=True)
    yc = y - mu
    var = jnp.mean(yc * yc, axis=-1, keepdims=True)
    return yc * lax.rsqrt(var + LN_EPS) * g + b


def _dot(a, b):
    return jnp.dot(a, b, preferred_element_type=_F32)


def _dot_nt(a, b):
    return lax.dot_general(a, b, (((1,), (1,)), ((), ())), preferred_element_type=_F32)


def _split_bf16(a):
    hi = lax.bitcast_convert_type(lax.bitcast_convert_type(a, jnp.uint32) & jnp.uint32(0xFFFF0000), _F32)
    return jnp.concatenate([hi.astype(_BF16), (a - hi).astype(_BF16)], axis=1)


def _softplus(z):
    neg_abs = lax.bitcast_convert_type(lax.bitcast_convert_type(z, jnp.uint32) | jnp.uint32(0x80000000), _F32)
    return jnp.maximum(z, 0.0) + jnp.log(1.0 + jnp.exp(neg_abs))


def _head_rms_gain(o, head0, gain):
    ss = o * o
    s0 = jnp.sum(jnp.where(head0, ss, 0.0), axis=1, keepdims=True)
    s1 = jnp.sum(jnp.where(head0, 0.0, ss), axis=1, keepdims=True)
    r = jnp.where(head0, lax.rsqrt(s0 / HEAD_DIM + RMS_EPS), lax.rsqrt(s1 / HEAD_DIM + RMS_EPS))
    return o * r * gain


def _swiglu_ln_rows(rows, x, xb_ref, wg_ref, wu_ref, wd_ref, g_ref, b_ref, o_ref):
    xb_ref[rows, :] = x.astype(_BF16)
    for f in range(N_FF_TILES):
        cols = slice(f * FF_TILE, (f + 1) * FF_TILE)
        xb = xb_ref[rows, :]
        gate = _dot(xb, wg_ref[:, cols])
        up = _dot(xb, wu_ref[:, cols])
        hidden = (gate * jax.nn.sigmoid(gate) * up).astype(_BF16)
        part = _dot(hidden, wd_ref[cols, :])
        if f == 0:
            o_ref[rows, :] = part
        else:
            o_ref[rows, :] += part
    o_ref[rows, :] = _layer_norm(ALPHA * x + o_ref[rows, :], g_ref[...], b_ref[...])


def _subtile_rows():
    return [slice(t * ROW_TILE, (t + 1) * ROW_TILE) for t in range(FFN_SUBTILES)]


def _ffn_ln_proj_kernel(x_ref, wg_ref, wu_ref, wd_ref, g_ref, b_ref, win_ref, o_ref, h_ref, xb_ref):
    for rows in _subtile_rows():
        _swiglu_ln_rows(rows, x_ref[rows, :], xb_ref, wg_ref, wu_ref, wd_ref, g_ref, b_ref, o_ref)
    slabs_per_tile = IN_PROJ_TILE // LANES
    for rows in _subtile_rows():
        xb_ref[rows, :] = o_ref[rows, :].astype(_BF16)
        for c in range(IN_PROJ_COLS // IN_PROJ_TILE):
            cols = slice(c * IN_PROJ_TILE, (c + 1) * IN_PROJ_TILE)
            res = _dot(xb_ref[rows, :], win_ref[:, cols]).astype(_BF16)
            for j in range(slabs_per_tile):
                h_ref[c * slabs_per_tile + j, rows, :] = res[:, j * LANES:(j + 1) * LANES]


def _mix_ffn_ln_kernel(x_ref, ma_ref, mb_ref, wo_ref, g2_ref, b2_ref, wg_ref, wu_ref, wd_ref, g3_ref, b3_ref,
                       o_ref, x2_ref, xb_ref):
    for rows in _subtile_rows():
        slabs = [ma_ref[p, rows, :] for p in range(D_SB // LANES)] + [mb_ref[p, rows, :] for p in range(D_CA // LANES)]
        y = ALPHA * x_ref[rows, :] + _dot(jnp.concatenate(slabs, axis=1), wo_ref[...])
        x2_ref[rows, :] = _layer_norm(y, g2_ref[...], b2_ref[...])
    for rows in _subtile_rows():
        _swiglu_ln_rows(rows, x2_ref[rows, :], xb_ref, wg_ref, wu_ref, wd_ref, g3_ref, b3_ref, o_ref)


IN_PROJ_COLS = 3 * (D_SB + D_CA)
IN_PROJ_TILE = 512
FFN_STEP_ROWS = FFN_SUBTILES * ROW_TILE
_CONST2 = lambda i: (0, 0)
_RESIDENT = dict(pipeline_mode=pl.Buffered(1))
_ROWS_SPEC = pl.BlockSpec((FFN_STEP_ROWS, D_MODEL), lambda i: (i, 0))
_FFN_WEIGHT_SPECS = [
    pl.BlockSpec((D_MODEL, D_FF), _CONST2, **_RESIDENT),
    pl.BlockSpec((D_MODEL, D_FF), _CONST2, **_RESIDENT),
    pl.BlockSpec((D_FF, D_MODEL), _CONST2, **_RESIDENT),
    pl.BlockSpec((1, D_MODEL), _CONST2),
    pl.BlockSpec((1, D_MODEL), _CONST2),
]
_ROWWISE_PARAMS = pltpu.CompilerParams(dimension_semantics=("arbitrary",), vmem_limit_bytes=VMEM_LIMIT_BYTES)


def _slab_spec(n_cols):
    return pl.BlockSpec((n_cols // LANES, FFN_STEP_ROWS, LANES), lambda i: (0, i, 0))


def _ffn_ln_proj(x, wg, wu, wd, g, b, w_in):
    rows = x.shape[0]
    assert rows % FFN_STEP_ROWS == 0
    return pl.pallas_call(
        _ffn_ln_proj_kernel,
        grid=(rows // FFN_STEP_ROWS,),
        in_specs=[_ROWS_SPEC] + _FFN_WEIGHT_SPECS + [pl.BlockSpec((D_MODEL, IN_PROJ_COLS), _CONST2, **_RESIDENT)],
        out_specs=[_ROWS_SPEC, _slab_spec(IN_PROJ_COLS)],
        out_shape=[jax.ShapeDtypeStruct((rows, D_MODEL), _F32),
                   jax.ShapeDtypeStruct((IN_PROJ_COLS // LANES, rows, LANES), _BF16)],
        scratch_shapes=[pltpu.VMEM((FFN_STEP_ROWS, D_MODEL), _BF16)],
        compiler_params=_ROWWISE_PARAMS,
        name="ffn_ln_proj",
    )(x, wg, wu, wd, g, b, w_in)


def _mix_ffn_ln(x, ma, mb, wo, g2, b2, wg, wu, wd, g3, b3):
    rows = x.shape[0]
    assert rows % FFN_STEP_ROWS == 0
    return pl.pallas_call(
        _mix_ffn_ln_kernel,
        grid=(rows // FFN_STEP_ROWS,),
        in_specs=[_ROWS_SPEC, _slab_spec(D_SB), _slab_spec(D_CA),
                  pl.BlockSpec((D_SB + D_CA, D_MODEL), _CONST2, **_RESIDENT),
                  pl.BlockSpec((1, D_MODEL), _CONST2), pl.BlockSpec((1, D_MODEL), _CONST2)] + _FFN_WEIGHT_SPECS,
        out_specs=_ROWS_SPEC,
        out_shape=jax.ShapeDtypeStruct((rows, D_MODEL), _F32),
        scratch_shapes=[pltpu.VMEM((FFN_STEP_ROWS, D_MODEL), _F32), pltpu.VMEM((FFN_STEP_ROWS, D_MODEL), _BF16)],
        compiler_params=_ROWWISE_PARAMS,
        name="mix_ffn_ln",
    )(x, ma, mb, wo, g2, b2, wg, wu, wd, g3, b3)


def _sb_cumsum_rhs():
    j = np.arange(SB_KBLOCK)[:, None]
    s = np.arange(SB_KBLOCK)[None, :]
    u = np.concatenate([(j >= s), np.ones((SB_KBLOCK, LANES), bool)], axis=1).astype(np.float32)
    return jnp.asarray(np.concatenate([u, u], axis=0), dtype=_BF16)


def _sb_kernel(q_ref, k_ref, v_ref, uu_ref, gain_ref, o_ref, kpad_ref, vpad_ref, acc_ref, spent_ref, flag_ref):
    seq = q_ref.shape[0]
    n_qblocks = seq // SB_QBLOCK
    kpad_ref[0:SB_PAD, :] = jnp.zeros((SB_PAD, LANES), _BF16)
    vpad_ref[0:SB_PAD, :] = jnp.zeros((SB_PAD, LANES), _BF16)
    kpad_ref[SB_PAD:, :] = k_ref[...]
    vpad_ref[SB_PAD:, :] = v_ref[...]

    rows2 = 2 * SB_QBLOCK
    row = lax.broadcasted_iota(jnp.int32, (rows2, LANES), 0)
    lane = lax.broadcasted_iota(jnp.int32, (rows2, LANES), 1)
    own_scale = jnp.where((row < SB_QBLOCK) == (lane < HEAD_DIM), ATTN_SCALE, 0.0).astype(_BF16)
    visible = lane - (SB_KBLOCK - SB_QBLOCK) < (row & (SB_QBLOCK - 1))
    head0 = lax.broadcasted_iota(jnp.int32, (SB_QBLOCK, LANES), 1) < HEAD_DIM
    gain = gain_ref[...]

    def windows(ms):
        r0s = [pl.multiple_of(m * SB_QBLOCK, SB_QBLOCK) for m in ms]
        q2s, zs, sps = [], [], []
        for r0 in r0s:
            q = q_ref[pl.ds(r0, SB_QBLOCK), :]
            q2s.append(jnp.concatenate([q, q], axis=0) * own_scale)
        for r0, q2 in zip(r0s, q2s):
            zs.append(_dot_nt(q2, kpad_ref[pl.ds(r0, SB_WINDOW), :]))
        zs = [(z[:, :SB_KBLOCK], jnp.where(visible, z[:, SB_KBLOCK:], MASK_BIAS)) for z in zs]
        for z_far, z_near in zs:
            sps.append(_softplus(z_far))
            sps.append(_softplus(z_near))
        cs = _dot(_split_bf16(jnp.concatenate(sps, axis=0)), uu_ref[...])
        ws, spents = [], []
        for g, (z_far, z_near) in enumerate(zs):
            far = cs[(2 * g) * rows2:(2 * g + 1) * rows2]
            near = cs[(2 * g + 1) * rows2:(2 * g + 2) * rows2]
            tot_near = near[:, SB_KBLOCK:]
            w_far = jnp.exp(z_far - (far[:, :SB_KBLOCK] + tot_near))
            w_near = jnp.exp(z_near - near[:, :SB_KBLOCK])
            ws.append(jnp.concatenate([w_far, w_near], axis=1).astype(_BF16))
            spents.append(far[:, SB_KBLOCK:] + tot_near)
        accs = [_dot(w, vpad_ref[pl.ds(r0, SB_WINDOW), :]) for r0, w in zip(r0s, ws)]
        return r0s, accs, spents

    def finish(r0, acc):
        o = jnp.where(head0, acc[:SB_QBLOCK], acc[SB_QBLOCK:])
        o_ref[pl.ds(r0, SB_QBLOCK), :] = _head_rms_gain(o, head0, gain).astype(_BF16)

    def first_pass(it, carry):
        ms = [it * SB_GROUP + g for g in range(SB_GROUP)]
        r0s, accs, spents = windows(ms)
        for m, r0, acc, spent in zip(ms, r0s, accs, spents):
            finish(r0, acc)
            a0 = pl.multiple_of(m * rows2, rows2)
            acc_ref[pl.ds(a0, rows2), :] = acc
            spent_ref[pl.ds(a0, rows2), :] = spent
            flag_ref[m] = jnp.min(spent)
        return carry

    lax.fori_loop(0, n_qblocks // SB_GROUP, first_pass, 0)

    def second_pass(m, carry):
        @pl.when(flag_ref[m] <= SB_SKIP_LOG)
        def _():
            r0 = pl.multiple_of(m * SB_QBLOCK, SB_QBLOCK)
            a0 = pl.multiple_of(m * rows2, rows2)
            q = q_ref[pl.ds(r0, SB_QBLOCK), :]
            q2 = jnp.concatenate([q, q], axis=0) * own_scale

            def more(state):
                j, _, spent = state
                return jnp.logical_and(r0 - j * SB_KBLOCK > SB_PAD, jnp.min(spent) <= SB_SKIP_LOG)

            def key_block(state):
                j, acc, spent = state
                rr = pl.multiple_of(r0 - (j + 1) * SB_KBLOCK, SB_QBLOCK)
                kb = kpad_ref[pl.ds(rr, SB_KBLOCK), :]
                vb = vpad_ref[pl.ds(rr, SB_KBLOCK), :]
                z = _dot_nt(q2, kb)
                cs = _dot(_split_bf16(_softplus(z)), uu_ref[...])
                w = jnp.exp(z - (cs[:, :SB_KBLOCK] + spent))
                return j + 1, acc + _dot(w.astype(_BF16), vb), spent + cs[:, SB_KBLOCK:]

            start = (0, acc_ref[pl.ds(a0, rows2), :], spent_ref[pl.ds(a0, rows2), :])
            _, acc, _ = lax.while_loop(more, key_block, start)
            finish(r0, acc)
        return carry

    lax.fori_loop(0, n_qblocks, second_pass, 0)


def _sb_attn(h, gain, batch):
    rows = h.shape[1]
    seq = rows // batch
    n_pairs = D_SB // LANES
    col_blocks = D_SB // LANES
    uu = _sb_cumsum_rhs()
    return pl.pallas_call(
        _sb_kernel,
        grid=(batch, n_pairs),
        in_specs=[
            pl.BlockSpec((None, seq, LANES), lambda b, p: (p, b, 0)),
            pl.BlockSpec((None, seq, LANES), lambda b, p: (col_blocks + p, b, 0)),
            pl.BlockSpec((None, seq, LANES), lambda b, p: (2 * col_blocks + p, b, 0)),
            pl.BlockSpec(uu.shape, lambda b, p: (0, 0)),
            pl.BlockSpec((1, LANES), lambda b, p: (0, p)),
        ],
        out_specs=pl.BlockSpec((None, seq, LANES), lambda b, p: (p, b, 0)),
        out_shape=jax.ShapeDtypeStruct((n_pairs, rows, LANES), _BF16),
        scratch_shapes=[
            pltpu.VMEM((seq + SB_PAD, LANES), _BF16),
            pltpu.VMEM((seq + SB_PAD, LANES), _BF16),
            pltpu.VMEM((HEADS_PER_BLOCK * seq, LANES), _F32),
            pltpu.VMEM((HEADS_PER_BLOCK * seq, LANES), _F32),
            pltpu.SMEM((seq // SB_QBLOCK,), _F32),
        ],
        compiler_params=pltpu.CompilerParams(
            dimension_semantics=("arbitrary", "arbitrary"), vmem_limit_bytes=VMEM_LIMIT_BYTES),
        name="sb_attn",
    )(h, h, h, uu, gain)


def _ca_bias_table(rel_bias):
    n_heads = rel_bias.shape[0]
    row_len = CA_WINDOW + CA_PAIR
    period = row_len + 1
    n_far = CA_PAD - MAX_REL + 1
    far = rel_bias[:, 2 * MAX_REL:]
    ramp = rel_bias[:, 1:2 * MAX_REL][:, ::-1]
    by_offset = jnp.concatenate([
        jnp.broadcast_to(far, (n_heads, n_far)), ramp,
        jnp.broadcast_to(far, (n_heads, period - n_far - ramp.shape[1]))], axis=1)
    table = jnp.tile(by_offset, (1, CA_PAIR))[:, :CA_PAIR * row_len]
    table = table.reshape(n_heads, CA_PAIR, row_len)[:, :, :CA_WINDOW]
    qi = np.arange(CA_PAIR)[:, None]
    kj = np.arange(CA_WINDOW)[None, :]
    visible = np.where(qi < CHUNK, kj < CA_PAD + CHUNK, kj >= CHUNK)
    table = jnp.where(jnp.asarray(visible)[None], table.astype(_F32), MASK_BIAS)
    return table.reshape(-1, CA_WINDOW)


def _ca_kernel(q_ref, k_ref, v_ref, bias_ref, gain_ref, o_ref, kpad_ref, vpad_ref):
    seq = q_ref.shape[0]
    n_pairs = seq // CA_PAIR
    kpad_ref[0:CA_PAD, :] = jnp.zeros((CA_PAD, LANES), _BF16)
    vpad_ref[0:CA_PAD, :] = jnp.zeros((CA_PAD, LANES), _BF16)
    kpad_ref[CA_PAD:, :] = k_ref[...]
    vpad_ref[CA_PAD:, :] = v_ref[...]

    rows2 = 2 * CA_PAIR
    row = lax.broadcasted_iota(jnp.int32, (rows2, LANES), 0)
    lane = lax.broadcasted_iota(jnp.int32, (rows2, LANES), 1)
    own_scale = jnp.where((row < CA_PAIR) == (lane < HEAD_DIM), ATTN_SCALE, 0.0).astype(_BF16)
    head0 = lax.broadcasted_iota(jnp.int32, (CA_PAIR, LANES), 1) < HEAD_DIM
    gain = gain_ref[...]

    def pairs(c2s, first_group):
        if first_group:
            r0s = [c2 * CA_PAIR for c2 in c2s]
        else:
            r0s = [pl.multiple_of(c2 * CA_PAIR, CA_PAIR) for c2 in c2s]
        q2s, scs, ps, denoms = [], [], [], []
        for r0 in r0s:
            q = q_ref[pl.ds(r0, CA_PAIR), :]
            q2s.append(jnp.concatenate([q, q], axis=0) * own_scale)
        for r0, q2 in zip(r0s, q2s):
            sc = _dot_nt(q2, kpad_ref[pl.ds(r0, CA_WINDOW), :]) + bias_ref[...]
            if first_group and r0 < CA_PAD:
                col = lax.broadcasted_iota(jnp.int32, (rows2, CA_WINDOW), 1)
                sc = jnp.where(col >= CA_PAD - r0, sc, MASK_BIAS)
            scs.append(sc)
        for sc in scs:
            p = jnp.exp(sc - jnp.max(sc, axis=1, keepdims=True))
            denoms.append(jnp.sum(p, axis=1, keepdims=True))
            ps.append(p.astype(_BF16))
        accs = [_dot(p, vpad_ref[pl.ds(r0, CA_WINDOW), :]) for r0, p in zip(r0s, ps)]
        for r0, acc, denom in zip(r0s, accs, denoms):
            acc = acc / denom
            o = jnp.where(head0, acc[:CA_PAIR], acc[CA_PAIR:])
            o_ref[pl.ds(r0, CA_PAIR), :] = _head_rms_gain(o, head0, gain).astype(_BF16)

    pairs(list(range(CA_GROUP)), True)

    def later_pairs(it, carry):
        pairs([it * CA_GROUP + g for g in range(CA_GROUP)], False)
        return carry

    lax.fori_loop(1, n_pairs // CA_GROUP, later_pairs, 0)


def _ca_attn(h, bias, gain, batch):
    rows = h.shape[1]
    seq = rows // batch
    n_pairs = D_CA // LANES
    first = 3 * D_SB // LANES
    col_blocks = D_CA // LANES
    return pl.pallas_call(
        _ca_kernel,
        grid=(batch, n_pairs),
        in_specs=[
            pl.BlockSpec((None, seq, LANES), lambda b, p: (first + p, b, 0)),
            pl.BlockSpec((None, seq, LANES), lambda b, p: (first + col_blocks + p, b, 0)),
            pl.BlockSpec((None, seq, LANES), lambda b, p: (first + 2 * col_blocks + p, b, 0)),
            pl.BlockSpec((HEADS_PER_BLOCK * CA_PAIR, CA_WINDOW), lambda b, p: (p, 0)),
            pl.BlockSpec((1, LANES), lambda b, p: (0, p)),
        ],
        out_specs=pl.BlockSpec((None, seq, LANES), lambda b, p: (p, b, 0)),
        out_shape=jax.ShapeDtypeStruct((n_pairs, rows, LANES), _BF16),
        scratch_shapes=[
            pltpu.VMEM((seq + CA_PAD, LANES), _BF16),
            pltpu.VMEM((seq + CA_PAD, LANES), _BF16),
        ],
        compiler_params=pltpu.CompilerParams(
            dimension_semantics=("arbitrary", "arbitrary"), vmem_limit_bytes=VMEM_LIMIT_BYTES),
        name="ca_attn",
    )(h, h, h, bias, gain)


def _half_bf16(w):
    return (FFN_RESIDUAL * w).astype(_BF16)


def kernel(x, ffn1_w_gate, ffn1_w_up, ffn1_w_down, ln1_g, ln1_b, w_in, rel_bias, sb_out_g, ca_out_g, w_out,
           ln2_g, ln2_b, ffn2_w_gate, ffn2_w_up, ffn2_w_down, ln3_g, ln3_b):
    batch, seq, _ = x.shape
    assert CA_GROUP * CA_PAIR >= CA_PAD and seq % (CA_PAIR * CA_GROUP) == 0 and seq % (SB_QBLOCK * SB_GROUP) == 0 and (batch * seq) % (FFN_SUBTILES * ROW_TILE) == 0
    row = lambda a: a.reshape(1, -1)
    xf = x.reshape(batch * seq, D_MODEL)
    for l in range(DEPTH):
        xf, h = _ffn_ln_proj(xf, ffn1_w_gate[l].astype(_BF16), ffn1_w_up[l].astype(_BF16), _half_bf16(ffn1_w_down[l]),
                             row(ln1_g[l]), row(ln1_b[l]), w_in[l].astype(_BF16))
        ma = _sb_attn(h, row(sb_out_g[l]), batch)
        mb = _ca_attn(h, _ca_bias_table(rel_bias[l]), row(ca_out_g[l]), batch)
        xf = _mix_ffn_ln(xf, ma, mb, w_out[l].astype(_BF16), row(ln2_g[l]), row(ln2_b[l]),
                         ffn2_w_gate[l].astype(_BF16), ffn2_w_up[l].astype(_BF16), _half_bf16(ffn2_w_down[l]),
                         row(ln3_g[l]), row(ln3_b[l]))
    return xf.reshape(batch, seq, D_MODEL)
```

```python
import math

import numpy as np
import jax
import jax.numpy as jnp
from jax import lax
from jax.experimental import pallas as pl
from jax.experimental.pallas import tpu as pltpu

D_MODEL = 1024
DEPTH = 4
HEAD_DIM = 64
D_SB = 512
D_CA = 512
CHUNK = 64
N_PREV_CHUNKS = 8
MAX_REL = 128
D_FF = 2816
FFN_RESIDUAL = 0.5
ALPHA = (2 * DEPTH) ** 0.25
LN_EPS = 1e-5
RMS_EPS = 1e-6
ATTN_SCALE = 1.0 / math.sqrt(HEAD_DIM)

LANES = 128
HEADS_PER_BLOCK = LANES // HEAD_DIM
VMEM_LIMIT_BYTES = 56 * 1024 * 1024

ROW_TILE = 512
FFN_SUBTILES = 2
FF_TILE = 256
N_FF_TILES = D_FF // FF_TILE
SB_SLABS = 2
SB_QBLOCK = 64
SB_KBLOCK = LANES
SB_WINDOW = 2 * SB_KBLOCK
SB_PAD = SB_WINDOW - SB_QBLOCK
SB_GROUP = 32
CA_GROUP = 16
CA_PAIR = 2 * CHUNK
CA_PAD = N_PREV_CHUNKS * CHUNK
CA_WINDOW = CA_PAD + CA_PAIR
SB_SKIP_LOG = 88.0
MASK_BIAS = -1e30

_F32 = jnp.float32
_BF16 = jnp.bfloat16


def _layer_norm(y, g, b):
    mu = jnp.mean(y, axis=-1, keepdims=True)
    yc = y - mu
    var = jnp.mean(yc * yc, axis=-1, keepdims=True)
    return yc * lax.rsqrt(var + LN_EPS) * g + b


def _dot(a, b):
    return jnp.dot(a, b, preferred_element_type=_F32)


def _dot_nt(a, b):
    return lax.dot_general(a, b, (((1,), (1,)), ((), ())), preferred_element_type=_F32)


def _split_bf16(a):
    hi = lax.bitcast_convert_type(lax.bitcast_convert_type(a, jnp.uint32) & jnp.uint32(0xFFFF0000), _F32)
    return jnp.concatenate([hi.astype(_BF16), (a - hi).astype(_BF16)], axis=1)


def _softplus(z):
    neg_abs = lax.bitcast_convert_type(lax.bitcast_convert_type(z, jnp.uint32) | jnp.uint32(0x80000000), _F32)
    return jnp.maximum(z, 0.0) + jnp.log(1.0 + jnp.exp(neg_abs))


def _head_rms_gain(o, head0, gain_rt_d):
    ss = o * o
    s0 = jnp.sum(jnp.where(head0, ss, 0.0), axis=1, keepdims=True)
    s1 = jnp.sum(jnp.where(head0, 0.0, ss), axis=1, keepdims=True)
    r = jnp.where(head0, lax.rsqrt(s0 + HEAD_DIM * RMS_EPS), lax.rsqrt(s1 + HEAD_DIM * RMS_EPS))
    return o * r * gain_rt_d


def _swiglu_ln_rows(rows, x, xb_ref, wg_ref, wu_ref, wd_ref, g_ref, b_ref, o_ref):
    xb_ref[rows, :] = x.astype(_BF16)
    for f in range(N_FF_TILES):
        cols = slice(f * FF_TILE, (f + 1) * FF_TILE)
        xb = xb_ref[rows, :]
        gate = _dot(xb, wg_ref[:, cols])
        up = _dot(xb, wu_ref[:, cols])
        hidden = (gate * jax.nn.sigmoid(gate) * up).astype(_BF16)
        part = _dot(hidden, wd_ref[cols, :])
        if f == 0:
            o_ref[rows, :] = part
        else:
            o_ref[rows, :] += part
    o_ref[rows, :] = _layer_norm(ALPHA * x + o_ref[rows, :], g_ref[...], b_ref[...])


def _subtile_rows():
    return [slice(t * ROW_TILE, (t + 1) * ROW_TILE) for t in range(FFN_SUBTILES)]


def _ffn_ln_proj_kernel(x_ref, wg_ref, wu_ref, wd_ref, g_ref, b_ref, win_ref, o_ref, h_ref, xb_ref):
    for rows in _subtile_rows():
        _swiglu_ln_rows(rows, x_ref[rows, :], xb_ref, wg_ref, wu_ref, wd_ref, g_ref, b_ref, o_ref)
    slabs_per_tile = IN_PROJ_TILE // LANES
    for rows in _subtile_rows():
        xb_ref[rows, :] = o_ref[rows, :].astype(_BF16)
        for c in range(IN_PROJ_COLS // IN_PROJ_TILE):
            cols = slice(c * IN_PROJ_TILE, (c + 1) * IN_PROJ_TILE)
            res = _dot(xb_ref[rows, :], win_ref[:, cols]).astype(_BF16)
            for j in range(slabs_per_tile):
                h_ref[c * slabs_per_tile + j, rows, :] = res[:, j * LANES:(j + 1) * LANES]


def _mix_ffn_ln_kernel(x_ref, ma_ref, mb_ref, wo_ref, g2_ref, b2_ref, wg_ref, wu_ref, wd_ref, g3_ref, b3_ref,
                       o_ref, x2_ref, xb_ref):
    for rows in _subtile_rows():
        slabs = [ma_ref[p, rows, :] for p in range(D_SB // LANES)] + [mb_ref[p, rows, :] for p in range(D_CA // LANES)]
        y = ALPHA * x_ref[rows, :] + _dot(jnp.concatenate(slabs, axis=1), wo_ref[...])
        x2_ref[rows, :] = _layer_norm(y, g2_ref[...], b2_ref[...])
    for rows in _subtile_rows():
        _swiglu_ln_rows(rows, x2_ref[rows, :], xb_ref, wg_ref, wu_ref, wd_ref, g3_ref, b3_ref, o_ref)


IN_PROJ_COLS = 3 * (D_SB + D_CA)
IN_PROJ_TILE = 512
FFN_STEP_ROWS = FFN_SUBTILES * ROW_TILE
_CONST2 = lambda i: (0, 0)
_RESIDENT = dict(pipeline_mode=pl.Buffered(1))
_ROWS_SPEC = pl.BlockSpec((FFN_STEP_ROWS, D_MODEL), lambda i: (i, 0))
_FFN_WEIGHT_SPECS = [
    pl.BlockSpec((D_MODEL, D_FF), _CONST2, **_RESIDENT),
    pl.BlockSpec((D_MODEL, D_FF), _CONST2, **_RESIDENT),
    pl.BlockSpec((D_FF, D_MODEL), _CONST2, **_RESIDENT),
    pl.BlockSpec((1, D_MODEL), _CONST2),
    pl.BlockSpec((1, D_MODEL), _CONST2),
]
_ROWWISE_PARAMS = pltpu.CompilerParams(dimension_semantics=("arbitrary",), vmem_limit_bytes=VMEM_LIMIT_BYTES)


def _slab_spec(n_cols):
    return pl.BlockSpec((n_cols // LANES, FFN_STEP_ROWS, LANES), lambda i: (0, i, 0))


def _ffn_ln_proj(x, wg, wu, wd, g, b, w_in):
    rows = x.shape[0]
    assert rows % FFN_STEP_ROWS == 0
    return pl.pallas_call(
        _ffn_ln_proj_kernel,
        grid=(rows // FFN_STEP_ROWS,),
        in_specs=[_ROWS_SPEC] + _FFN_WEIGHT_SPECS + [pl.BlockSpec((D_MODEL, IN_PROJ_COLS), _CONST2, **_RESIDENT)],
        out_specs=[_ROWS_SPEC, _slab_spec(IN_PROJ_COLS)],
        out_shape=[jax.ShapeDtypeStruct((rows, D_MODEL), _F32),
                   jax.ShapeDtypeStruct((IN_PROJ_COLS // LANES, rows, LANES), _BF16)],
        scratch_shapes=[pltpu.VMEM((FFN_STEP_ROWS, D_MODEL), _BF16)],
        compiler_params=_ROWWISE_PARAMS,
        name="ffn_ln_proj",
    )(x, wg, wu, wd, g, b, w_in)


def _mix_ffn_ln(x, ma, mb, wo, g2, b2, wg, wu, wd, g3, b3):
    rows = x.shape[0]
    assert rows % FFN_STEP_ROWS == 0
    return pl.pallas_call(
        _mix_ffn_ln_kernel,
        grid=(rows // FFN_STEP_ROWS,),
        in_specs=[_ROWS_SPEC, _slab_spec(D_SB), _slab_spec(D_CA),
                  pl.BlockSpec((D_SB + D_CA, D_MODEL), _CONST2, **_RESIDENT),
                  pl.BlockSpec((1, D_MODEL), _CONST2), pl.BlockSpec((1, D_MODEL), _CONST2)] + _FFN_WEIGHT_SPECS,
        out_specs=_ROWS_SPEC,
        out_shape=jax.ShapeDtypeStruct((rows, D_MODEL), _F32),
        scratch_shapes=[pltpu.VMEM((FFN_STEP_ROWS, D_MODEL), _F32), pltpu.VMEM((FFN_STEP_ROWS, D_MODEL), _BF16)],
        compiler_params=_ROWWISE_PARAMS,
        name="mix_ffn_ln",
    )(x, ma, mb, wo, g2, b2, wg, wu, wd, g3, b3)


def _sb_cumsum_rhs():
    j = np.arange(SB_KBLOCK)[:, None]
    s = np.arange(SB_KBLOCK)[None, :]
    u = np.concatenate([(j >= s), np.ones((SB_KBLOCK, LANES), bool)], axis=1).astype(np.float32)
    return jnp.asarray(np.concatenate([u, u], axis=0), dtype=_BF16)


def _sb_pair(q_ref, k_ref, v_ref, uu_ref, gain_ref, o_ref, kpad_ref, vpad_ref, acc_ref, spent_ref, flag_ref):
    seq = q_ref.shape[0]
    n_qblocks = seq // SB_QBLOCK
    kpad_ref[0:SB_PAD, :] = jnp.zeros((SB_PAD, LANES), _BF16)
    vpad_ref[0:SB_PAD, :] = jnp.zeros((SB_PAD, LANES), _BF16)
    kpad_ref[SB_PAD:, :] = k_ref[...]
    vpad_ref[SB_PAD:, :] = v_ref[...]

    rows2 = 2 * SB_QBLOCK
    row = lax.broadcasted_iota(jnp.int32, (rows2, LANES), 0)
    lane = lax.broadcasted_iota(jnp.int32, (rows2, LANES), 1)
    own_scale = jnp.where((row < SB_QBLOCK) == (lane < HEAD_DIM), ATTN_SCALE, 0.0).astype(_BF16)
    visible = lane - (SB_KBLOCK - SB_QBLOCK) < (row & (SB_QBLOCK - 1))
    head0 = lax.broadcasted_iota(jnp.int32, (SB_QBLOCK, LANES), 1) < HEAD_DIM
    gain = gain_ref[...] * math.sqrt(HEAD_DIM)

    def windows(ms):
        r0s = [pl.multiple_of(m * SB_QBLOCK, SB_QBLOCK) for m in ms]
        q2s, zs, sps = [], [], []
        for r0 in r0s:
            q = q_ref[pl.ds(r0, SB_QBLOCK), :]
            q2s.append(jnp.concatenate([q, q], axis=0) * own_scale)
        for r0, q2 in zip(r0s, q2s):
            zs.append(_dot_nt(q2, kpad_ref[pl.ds(r0, SB_WINDOW), :]))
        zs = [(z[:, :SB_KBLOCK], jnp.where(visible, z[:, SB_KBLOCK:], MASK_BIAS)) for z in zs]
        for z_far, z_near in zs:
            sps.append(_softplus(z_far))
            sps.append(_softplus(z_near))
        cs = _dot(_split_bf16(jnp.concatenate(sps, axis=0)), uu_ref[...])
        ws, spents = [], []
        for g, (z_far, z_near) in enumerate(zs):
            far = cs[(2 * g) * rows2:(2 * g + 1) * rows2]
            near = cs[(2 * g + 1) * rows2:(2 * g + 2) * rows2]
            tot_near = near[:, SB_KBLOCK:]
            w_far = jnp.exp(z_far - (far[:, :SB_KBLOCK] + tot_near))
            w_near = jnp.exp(z_near - near[:, :SB_KBLOCK])
            ws.append(jnp.concatenate([w_far, w_near], axis=1).astype(_BF16))
            spents.append(far[:, SB_KBLOCK:] + tot_near)
        accs = [_dot(w, vpad_ref[pl.ds(r0, SB_WINDOW), :]) for r0, w in zip(r0s, ws)]
        return r0s, accs, spents

    def finish(r0, acc):
        o = jnp.where(head0, acc[:SB_QBLOCK], acc[SB_QBLOCK:])
        o_ref[pl.ds(r0, SB_QBLOCK), :] = _head_rms_gain(o, head0, gain).astype(_BF16)

    def first_pass(it, carry):
        ms = [it * SB_GROUP + g for g in range(SB_GROUP)]
        r0s, accs, spents = windows(ms)
        for m, r0, acc, spent in zip(ms, r0s, accs, spents):
            finish(r0, acc)
            a0 = pl.multiple_of(m * rows2, rows2)
            acc_ref[pl.ds(a0, rows2), :] = acc
            spent_ref[pl.ds(a0, rows2), :] = spent
            flag_ref[m] = jnp.min(spent)
        return carry

    lax.fori_loop(0, n_qblocks // SB_GROUP, first_pass, 0)

    def second_pass(m, carry):
        @pl.when(flag_ref[m] <= SB_SKIP_LOG)
        def _():
            r0 = pl.multiple_of(m * SB_QBLOCK, SB_QBLOCK)
            a0 = pl.multiple_of(m * rows2, rows2)
            q = q_ref[pl.ds(r0, SB_QBLOCK), :]
            q2 = jnp.concatenate([q, q], axis=0) * own_scale

            def more(state):
                j, _, spent = state
                return jnp.logical_and(r0 - j * SB_KBLOCK > SB_PAD, jnp.min(spent) <= SB_SKIP_LOG)

            def key_block(state):
                j, acc, spent = state
                rr = pl.multiple_of(r0 - (j + 1) * SB_KBLOCK, SB_QBLOCK)
                kb = kpad_ref[pl.ds(rr, SB_KBLOCK), :]
                vb = vpad_ref[pl.ds(rr, SB_KBLOCK), :]
                z = _dot_nt(q2, kb)
                cs = _dot(_split_bf16(_softplus(z)), uu_ref[...])
                w = jnp.exp(z - (cs[:, :SB_KBLOCK] + spent))
                return j + 1, acc + _dot(w.astype(_BF16), vb), spent + cs[:, SB_KBLOCK:]

            start = (0, acc_ref[pl.ds(a0, rows2), :], spent_ref[pl.ds(a0, rows2), :])
            _, acc, _ = lax.while_loop(more, key_block, start)
            finish(r0, acc)
        return carry

    lax.fori_loop(0, n_qblocks, second_pass, 0)


def _sb_kernel(q_ref, k_ref, v_ref, uu_ref, gain_ref, o_ref, *scratch):
    for pp in range(SB_SLABS):
        _sb_pair(q_ref.at[pp], k_ref.at[pp], v_ref.at[pp], uu_ref, gain_ref.at[pp], o_ref.at[pp], *scratch)


def _sb_attn(h, gain, batch):
    rows = h.shape[1]
    seq = rows // batch
    n_pairs = D_SB // LANES
    col_blocks = D_SB // LANES
    uu = _sb_cumsum_rhs()
    return pl.pallas_call(
        _sb_kernel,
        grid=(batch, n_pairs // SB_SLABS),
        in_specs=[
            pl.BlockSpec((SB_SLABS, seq, LANES), lambda b, p: (p, b, 0)),
            pl.BlockSpec((SB_SLABS, seq, LANES), lambda b, p: (col_blocks // SB_SLABS + p, b, 0)),
            pl.BlockSpec((SB_SLABS, seq, LANES), lambda b, p: (2 * col_blocks // SB_SLABS + p, b, 0)),
            pl.BlockSpec(uu.shape, lambda b, p: (0, 0)),
            pl.BlockSpec((SB_SLABS, 1, LANES), lambda b, p: (p, 0, 0)),
        ],
        out_specs=pl.BlockSpec((SB_SLABS, seq, LANES), lambda b, p: (p, b, 0)),
        out_shape=jax.ShapeDtypeStruct((n_pairs, rows, LANES), _BF16),
        scratch_shapes=[
            pltpu.VMEM((seq + SB_PAD, LANES), _BF16),
            pltpu.VMEM((seq + SB_PAD, LANES), _BF16),
            pltpu.VMEM((HEADS_PER_BLOCK * seq, LANES), _F32),
            pltpu.VMEM((HEADS_PER_BLOCK * seq, LANES), _F32),
            pltpu.SMEM((seq // SB_QBLOCK,), _F32),
        ],
        compiler_params=pltpu.CompilerParams(
            dimension_semantics=("arbitrary", "arbitrary"), vmem_limit_bytes=VMEM_LIMIT_BYTES),
        name="sb_attn",
    )(h, h, h, uu, gain.reshape(n_pairs, 1, LANES))


def _ca_bias_table(rel_bias):
    n_heads = rel_bias.shape[0]
    row_len = CA_WINDOW + CA_PAIR
    period = row_len + 1
    n_far = CA_PAD - MAX_REL + 1
    far = rel_bias[:, 2 * MAX_REL:]
    ramp = rel_bias[:, 1:2 * MAX_REL][:, ::-1]
    by_offset = jnp.concatenate([
        jnp.broadcast_to(far, (n_heads, n_far)), ramp,
        jnp.broadcast_to(far, (n_heads, period - n_far - ramp.shape[1]))], axis=1)
    table = jnp.tile(by_offset, (1, CA_PAIR))[:, :CA_PAIR * row_len]
    table = table.reshape(n_heads, CA_PAIR, row_len)[:, :, :CA_WINDOW]
    qi = np.arange(CA_PAIR)[:, None]
    kj = np.arange(CA_WINDOW)[None, :]
    visible = np.where(qi < CHUNK, kj < CA_PAD + CHUNK, kj >= CHUNK)
    table = jnp.where(jnp.asarray(visible)[None], table.astype(_F32), MASK_BIAS)
    return table.reshape(-1, CA_WINDOW)


def _ca_kernel(q_ref, k_ref, v_ref, bias_ref, gain_ref, o_ref, kpad_ref, vpad_ref):
    seq = q_ref.shape[0]
    n_pairs = seq // CA_PAIR
    kpad_ref[0:CA_PAD, :] = jnp.zeros((CA_PAD, LANES), _BF16)
    vpad_ref[0:CA_PAD, :] = jnp.zeros((CA_PAD, LANES), _BF16)
    kpad_ref[CA_PAD:, :] = k_ref[...]
    vpad_ref[CA_PAD:, :] = v_ref[...]

    rows2 = 2 * CA_PAIR
    row = lax.broadcasted_iota(jnp.int32, (rows2, LANES), 0)
    lane = lax.broadcasted_iota(jnp.int32, (rows2, LANES), 1)
    own_scale = jnp.where((row < CA_PAIR) == (lane < HEAD_DIM), ATTN_SCALE, 0.0).astype(_BF16)
    head0 = lax.broadcasted_iota(jnp.int32, (CA_PAIR, LANES), 1) < HEAD_DIM
    gain = gain_ref[...] * math.sqrt(HEAD_DIM)

    def pairs(c2s, first_group):
        if first_group:
            r0s = [c2 * CA_PAIR for c2 in c2s]
        else:
            r0s = [pl.multiple_of(c2 * CA_PAIR, CA_PAIR) for c2 in c2s]
        q2s, scs, ps, denoms = [], [], [], []
        for r0 in r0s:
            q = q_ref[pl.ds(r0, CA_PAIR), :]
            q2s.append(jnp.concatenate([q, q], axis=0) * own_scale)
        for r0, q2 in zip(r0s, q2s):
            sc = _dot_nt(q2, kpad_ref[pl.ds(r0, CA_WINDOW), :]) + bias_ref[...]
            if first_group and r0 < CA_PAD:
                col = lax.broadcasted_iota(jnp.int32, (rows2, CA_WINDOW), 1)
                sc = jnp.where(col >= CA_PAD - r0, sc, MASK_BIAS)
            scs.append(sc)
        for sc in scs:
            p = jnp.exp(sc - jnp.max(sc, axis=1, keepdims=True))
            denoms.append(jnp.sum(p, axis=1, keepdims=True))
            ps.append(p.astype(_BF16))
        accs = [_dot(p, vpad_ref[pl.ds(r0, CA_WINDOW), :]) for r0, p in zip(r0s, ps)]
        for r0, acc, denom in zip(r0s, accs, denoms):
            acc = acc / denom
            o = jnp.where(head0, acc[:CA_PAIR], acc[CA_PAIR:])
            o_ref[pl.ds(r0, CA_PAIR), :] = _head_rms_gain(o, head0, gain).astype(_BF16)

    pairs(list(range(CA_GROUP)), True)

    def later_pairs(it, carry):
        pairs([it * CA_GROUP + g for g in range(CA_GROUP)], False)
        return carry

    lax.fori_loop(1, n_pairs // CA_GROUP, later_pairs, 0)


def _ca_attn(h, bias, gain, batch):
    rows = h.shape[1]
    seq = rows // batch
    n_pairs = D_CA // LANES
    first = 3 * D_SB // LANES
    col_blocks = D_CA // LANES
    return pl.pallas_call(
        _ca_kernel,
        grid=(batch, n_pairs),
        in_specs=[
            pl.BlockSpec((None, seq, LANES), lambda b, p: (first + p, b, 0)),
            pl.BlockSpec((None, seq, LANES), lambda b, p: (first + col_blocks + p, b, 0)),
            pl.BlockSpec((None, seq, LANES), lambda b, p: (first + 2 * col_blocks + p, b, 0)),
            pl.BlockSpec((HEADS_PER_BLOCK * CA_PAIR, CA_WINDOW), lambda b, p: (p, 0)),
            pl.BlockSpec((1, LANES), lambda b, p: (0, p)),
        ],
        out_specs=pl.BlockSpec((None, seq, LANES), lambda b, p: (p, b, 0)),
        out_shape=jax.ShapeDtypeStruct((n_pairs, rows, LANES), _BF16),
        scratch_shapes=[
            pltpu.VMEM((seq + CA_PAD, LANES), _BF16),
            pltpu.VMEM((seq + CA_PAD, LANES), _BF16),
        ],
        compiler_params=pltpu.CompilerParams(
            dimension_semantics=("arbitrary", "arbitrary"), vmem_limit_bytes=VMEM_LIMIT_BYTES),
        name="ca_attn",
    )(h, h, h, bias, gain)


def _half_bf16(w):
    return (FFN_RESIDUAL * w).astype(_BF16)


def kernel(x, ffn1_w_gate, ffn1_w_up, ffn1_w_down, ln1_g, ln1_b, w_in, rel_bias, sb_out_g, ca_out_g, w_out,
           ln2_g, ln2_b, ffn2_w_gate, ffn2_w_up, ffn2_w_down, ln3_g, ln3_b):
    batch, seq, _ = x.shape
    assert CA_GROUP * CA_PAIR >= CA_PAD and seq % (CA_PAIR * CA_GROUP) == 0 and seq % (SB_QBLOCK * SB_GROUP) == 0 and (batch * seq) % (FFN_SUBTILES * ROW_TILE) == 0
    row = lambda a: a.reshape(1, -1)
    xf = x.reshape(batch * seq, D_MODEL)
    for l in range(DEPTH):
        xf, h = _ffn_ln_proj(xf, ffn1_w_gate[l].astype(_BF16), ffn1_w_up[l].astype(_BF16), _half_bf16(ffn1_w_down[l]),
                             row(ln1_g[l]), row(ln1_b[l]), w_in[l].astype(_BF16))
        ma = _sb_attn(h, row(sb_out_g[l]), batch)
        mb = _ca_attn(h, _ca_bias_table(rel_bias[l]), row(ca_out_g[l]), batch)
        xf = _mix_ffn_ln(xf, ma, mb, w_out[l].astype(_BF16), row(ln2_g[l]), row(ln2_b[l]),
                         ffn2_w_gate[l].astype(_BF16), ffn2_w_up[l].astype(_BF16), _half_bf16(ffn2_w_down[l]),
                         row(ln3_g[l]), row(ln3_b[l]))
    return xf.reshape(batch, seq, D_MODEL)
```

```python
import math

import numpy as np
import jax
import jax.numpy as jnp
from jax import lax
from jax.experimental import pallas as pl
from jax.experimental.pallas import tpu as pltpu

D_MODEL = 1024
DEPTH = 4
HEAD_DIM = 64
D_SB = 512
D_CA = 512
CHUNK = 64
N_PREV_CHUNKS = 8
MAX_REL = 128
D_FF = 2816
FFN_RESIDUAL = 0.5
ALPHA = (2 * DEPTH) ** 0.25
LN_EPS = 1e-5
RMS_EPS = 1e-6
ATTN_SCALE = 1.0 / math.sqrt(HEAD_DIM)

LANES = 128
HEADS_PER_BLOCK = LANES // HEAD_DIM
VMEM_LIMIT_BYTES = 56 * 1024 * 1024

ROW_TILE = 512
FFN_SUBTILES = 2
FF_TILE = 256
N_FF_TILES = D_FF // FF_TILE
SB_SLABS = 2
SB_QBLOCK = 64
SB_KBLOCK = LANES
SB_WINDOW = 2 * SB_KBLOCK
SB_PAD = SB_WINDOW - SB_QBLOCK
SB_GROUP = 32
CA_GROUP = 16
CA_PAIR = 2 * CHUNK
CA_PAD = N_PREV_CHUNKS * CHUNK
CA_WINDOW = CA_PAD + CA_PAIR
SB_SKIP_LOG = 88.0
MASK_BIAS = -1e30

_F32 = jnp.float32
_BF16 = jnp.bfloat16


def _layer_norm(y, g, b):
    mu = jnp.mean(y, axis=-1, keepdims=True)
    yc = y - mu
    var = jnp.mean(yc * yc, axis=-1, keepdims=True)
    return yc * lax.rsqrt(var + LN_EPS) * g + b


def _dot(a, b):
    return jnp.dot(a, b, preferred_element_type=_F32)


def _dot_nt(a, b):
    return lax.dot_general(a, b, (((1,), (1,)), ((), ())), preferred_element_type=_F32)


def _split_bf16(a):
    hi = lax.bitcast_convert_type(lax.bitcast_convert_type(a, jnp.uint32) & jnp.uint32(0xFFFF0000), _F32)
    return jnp.concatenate([hi.astype(_BF16), (a - hi).astype(_BF16)], axis=1)


def _softplus(z):
    neg_abs = lax.bitcast_convert_type(lax.bitcast_convert_type(z, jnp.uint32) | jnp.uint32(0x80000000), _F32)
    return jnp.maximum(z, 0.0) + jnp.log(1.0 + jnp.exp(neg_abs))


def _head_rms_gain(o, head0, gain_rt_d):
    ss = o * o
    s0 = jnp.sum(jnp.where(head0, ss, 0.0), axis=1, keepdims=True)
    s1 = jnp.sum(jnp.where(head0, 0.0, ss), axis=1, keepdims=True)
    r = jnp.where(head0, lax.rsqrt(s0 + HEAD_DIM * RMS_EPS), lax.rsqrt(s1 + HEAD_DIM * RMS_EPS))
    return o * r * gain_rt_d


def _swiglu_ln_rows(rows, x, xb_ref, wg_ref, wu_ref, wd_ref, g_ref, b_ref, o_ref):
    xb_ref[rows, :] = x.astype(_BF16)
    for f in range(N_FF_TILES):
        cols = slice(f * FF_TILE, (f + 1) * FF_TILE)
        xb = xb_ref[rows, :]
        gate = _dot(xb, wg_ref[:, cols])
        up = _dot(xb, wu_ref[:, cols])
        hidden = (gate * jax.nn.sigmoid(gate) * up).astype(_BF16)
        part = _dot(hidden, wd_ref[cols, :])
        if f == 0:
            o_ref[rows, :] = part
        else:
            o_ref[rows, :] += part
    o_ref[rows, :] = _layer_norm(ALPHA * x + o_ref[rows, :], g_ref[...], b_ref[...])


def _subtile_rows():
    return [slice(t * ROW_TILE, (t + 1) * ROW_TILE) for t in range(FFN_SUBTILES)]


def _ffn_ln_proj_kernel(x_ref, wg_ref, wu_ref, wd_ref, g_ref, b_ref, win_ref, o_ref, h_ref, xb_ref):
    for rows in _subtile_rows():
        _swiglu_ln_rows(rows, x_ref[rows, :], xb_ref, wg_ref, wu_ref, wd_ref, g_ref, b_ref, o_ref)
    slabs_per_tile = IN_PROJ_TILE // LANES
    for rows in _subtile_rows():
        xb_ref[rows, :] = o_ref[rows, :].astype(_BF16)
        for c in range(IN_PROJ_COLS // IN_PROJ_TILE):
            cols = slice(c * IN_PROJ_TILE, (c + 1) * IN_PROJ_TILE)
            res = _dot(xb_ref[rows, :], win_ref[:, cols]).astype(_BF16)
            for j in range(slabs_per_tile):
                h_ref[c * slabs_per_tile + j, rows, :] = res[:, j * LANES:(j + 1) * LANES]


def _mix_ffn_ln_kernel(x_ref, ma_ref, mb_ref, wo_ref, g2_ref, b2_ref, wg_ref, wu_ref, wd_ref, g3_ref, b3_ref,
                       o_ref, x2_ref, xb_ref):
    for rows in _subtile_rows():
        slabs = [ma_ref[p, rows, :] for p in range(D_SB // LANES)] + [mb_ref[p, rows, :] for p in range(D_CA // LANES)]
        y = ALPHA * x_ref[rows, :] + _dot(jnp.concatenate(slabs, axis=1), wo_ref[...])
        x2_ref[rows, :] = _layer_norm(y, g2_ref[...], b2_ref[...])
    for rows in _subtile_rows():
        _swiglu_ln_rows(rows, x2_ref[rows, :], xb_ref, wg_ref, wu_ref, wd_ref, g3_ref, b3_ref, o_ref)


IN_PROJ_COLS = 3 * (D_SB + D_CA)
IN_PROJ_TILE = 512
FFN_STEP_ROWS = FFN_SUBTILES * ROW_TILE
_RESIDENT = dict(pipeline_mode=pl.Buffered(1))
_ROWS_SPEC = pl.BlockSpec((FFN_STEP_ROWS, D_MODEL), lambda i: (i, 0))


def _layer_spec(shape, layer, resident=True):
    return pl.BlockSpec((None,) + shape, lambda i: (layer, 0, 0), **(_RESIDENT if resident else {}))


def _ffn_weight_specs(layer):
    return [
        _layer_spec((D_MODEL, D_FF), layer),
        _layer_spec((D_MODEL, D_FF), layer),
        _layer_spec((D_FF, D_MODEL), layer),
        _layer_spec((1, D_MODEL), layer, resident=False),
        _layer_spec((1, D_MODEL), layer, resident=False),
    ]

_ROWWISE_PARAMS = pltpu.CompilerParams(dimension_semantics=("arbitrary",), vmem_limit_bytes=VMEM_LIMIT_BYTES)


def _slab_spec(n_cols):
    return pl.BlockSpec((n_cols // LANES, FFN_STEP_ROWS, LANES), lambda i: (0, i, 0))


def _ffn_ln_proj(x, wg, wu, wd, g, b, w_in, layer):
    rows = x.shape[0]
    assert rows % FFN_STEP_ROWS == 0
    return pl.pallas_call(
        _ffn_ln_proj_kernel,
        grid=(rows // FFN_STEP_ROWS,),
        in_specs=[_ROWS_SPEC] + _ffn_weight_specs(layer) + [_layer_spec((D_MODEL, IN_PROJ_COLS), layer)],
        out_specs=[_ROWS_SPEC, _slab_spec(IN_PROJ_COLS)],
        out_shape=[jax.ShapeDtypeStruct((rows, D_MODEL), _F32),
                   jax.ShapeDtypeStruct((IN_PROJ_COLS // LANES, rows, LANES), _BF16)],
        scratch_shapes=[pltpu.VMEM((FFN_STEP_ROWS, D_MODEL), _BF16)],
        compiler_params=_ROWWISE_PARAMS,
        name="ffn_ln_proj",
    )(x, wg, wu, wd, g, b, w_in)


def _mix_ffn_ln(x, ma, mb, wo, g2, b2, wg, wu, wd, g3, b3, layer):
    rows = x.shape[0]
    assert rows % FFN_STEP_ROWS == 0
    return pl.pallas_call(
        _mix_ffn_ln_kernel,
        grid=(rows // FFN_STEP_ROWS,),
        in_specs=[_ROWS_SPEC, _slab_spec(D_SB), _slab_spec(D_CA),
                  _layer_spec((D_SB + D_CA, D_MODEL), layer),
                  _layer_spec((1, D_MODEL), layer, resident=False),
                  _layer_spec((1, D_MODEL), layer, resident=False)] + _ffn_weight_specs(layer),
        out_specs=_ROWS_SPEC,
        out_shape=jax.ShapeDtypeStruct((rows, D_MODEL), _F32),
        scratch_shapes=[pltpu.VMEM((FFN_STEP_ROWS, D_MODEL), _F32), pltpu.VMEM((FFN_STEP_ROWS, D_MODEL), _BF16)],
        compiler_params=_ROWWISE_PARAMS,
        name="mix_ffn_ln",
    )(x, ma, mb, wo, g2, b2, wg, wu, wd, g3, b3)


def _sb_cumsum_rhs():
    j = np.arange(SB_KBLOCK)[:, None]
    s = np.arange(SB_KBLOCK)[None, :]
    u = np.concatenate([(j >= s), np.ones((SB_KBLOCK, LANES), bool)], axis=1).astype(np.float32)
    return jnp.asarray(np.concatenate([u, u], axis=0), dtype=_BF16)


def _sb_pair(q_ref, k_ref, v_ref, uu_ref, gain_ref, o_ref, kpad_ref, vpad_ref, acc_ref, spent_ref, flag_ref):
    seq = q_ref.shape[0]
    n_qblocks = seq // SB_QBLOCK
    kpad_ref[0:SB_PAD, :] = jnp.zeros((SB_PAD, LANES), _BF16)
    vpad_ref[0:SB_PAD, :] = jnp.zeros((SB_PAD, LANES), _BF16)
    kpad_ref[SB_PAD:, :] = k_ref[...]
    vpad_ref[SB_PAD:, :] = v_ref[...]

    rows2 = 2 * SB_QBLOCK
    row = lax.broadcasted_iota(jnp.int32, (rows2, LANES), 0)
    lane = lax.broadcasted_iota(jnp.int32, (rows2, LANES), 1)
    own_scale = jnp.where((row < SB_QBLOCK) == (lane < HEAD_DIM), ATTN_SCALE, 0.0).astype(_BF16)
    visible = lane - (SB_KBLOCK - SB_QBLOCK) < (row & (SB_QBLOCK - 1))
    head0 = lax.broadcasted_iota(jnp.int32, (SB_QBLOCK, LANES), 1) < HEAD_DIM
    gain = gain_ref[...] * math.sqrt(HEAD_DIM)

    def windows(ms):
        r0s = [pl.multiple_of(m * SB_QBLOCK, SB_QBLOCK) for m in ms]
        q2s, zs, sps = [], [], []
        for r0 in r0s:
            q = q_ref[pl.ds(r0, SB_QBLOCK), :]
            q2s.append(jnp.concatenate([q, q], axis=0) * own_scale)
        for r0, q2 in zip(r0s, q2s):
            zs.append(_dot_nt(q2, kpad_ref[pl.ds(r0, SB_WINDOW), :]))
        zs = [(z[:, :SB_KBLOCK], jnp.where(visible, z[:, SB_KBLOCK:], MASK_BIAS)) for z in zs]
        for z_far, z_near in zs:
            sps.append(_softplus(z_far))
            sps.append(_softplus(z_near))
        cs = _dot(_split_bf16(jnp.concatenate(sps, axis=0)), uu_ref[...])
        ws, spents = [], []
        for g, (z_far, z_near) in enumerate(zs):
            far = cs[(2 * g) * rows2:(2 * g + 1) * rows2]
            near = cs[(2 * g + 1) * rows2:(2 * g + 2) * rows2]
            tot_near = near[:, SB_KBLOCK:]
            w_far = jnp.exp(z_far - (far[:, :SB_KBLOCK] + tot_near))
            w_near = jnp.exp(z_near - near[:, :SB_KBLOCK])
            ws.append(jnp.concatenate([w_far, w_near], axis=1).astype(_BF16))
            spents.append(far[:, SB_KBLOCK:] + tot_near)
        accs = [_dot(w, vpad_ref[pl.ds(r0, SB_WINDOW), :]) for r0, w in zip(r0s, ws)]
        return r0s, accs, spents

    def finish(r0, acc):
        o = jnp.where(head0, acc[:SB_QBLOCK], acc[SB_QBLOCK:])
        o_ref[pl.ds(r0, SB_QBLOCK), :] = _head_rms_gain(o, head0, gain).astype(_BF16)

    def first_pass(it, carry):
        ms = [it * SB_GROUP + g for g in range(SB_GROUP)]
        r0s, accs, spents = windows(ms)
        for m, r0, acc, spent in zip(ms, r0s, accs, spents):
            finish(r0, acc)
            a0 = pl.multiple_of(m * rows2, rows2)
            acc_ref[pl.ds(a0, rows2), :] = acc
            spent_ref[pl.ds(a0, rows2), :] = spent
            flag_ref[m] = jnp.min(spent)
        return carry

    lax.fori_loop(0, n_qblocks // SB_GROUP, first_pass, 0)

    def second_pass(m, carry):
        @pl.when(flag_ref[m] <= SB_SKIP_LOG)
        def _():
            r0 = pl.multiple_of(m * SB_QBLOCK, SB_QBLOCK)
            a0 = pl.multiple_of(m * rows2, rows2)
            q = q_ref[pl.ds(r0, SB_QBLOCK), :]
            q2 = jnp.concatenate([q, q], axis=0) * own_scale

            def more(state):
                j, _, spent = state
                return jnp.logical_and(r0 - j * SB_KBLOCK > SB_PAD, jnp.min(spent) <= SB_SKIP_LOG)

            def key_block(state):
                j, acc, spent = state
                rr = pl.multiple_of(r0 - (j + 1) * SB_KBLOCK, SB_QBLOCK)
                kb = kpad_ref[pl.ds(rr, SB_KBLOCK), :]
                vb = vpad_ref[pl.ds(rr, SB_KBLOCK), :]
                z = _dot_nt(q2, kb)
                cs = _dot(_split_bf16(_softplus(z)), uu_ref[...])
                w = jnp.exp(z - (cs[:, :SB_KBLOCK] + spent))
                return j + 1, acc + _dot(w.astype(_BF16), vb), spent + cs[:, SB_KBLOCK:]

            start = (0, acc_ref[pl.ds(a0, rows2), :], spent_ref[pl.ds(a0, rows2), :])
            _, acc, _ = lax.while_loop(more, key_block, start)
            finish(r0, acc)
        return carry

    lax.fori_loop(0, n_qblocks, second_pass, 0)


def _sb_kernel(q_ref, k_ref, v_ref, uu_ref, gain_ref, o_ref, *scratch):
    for pp in range(SB_SLABS):
        _sb_pair(q_ref.at[pp], k_ref.at[pp], v_ref.at[pp], uu_ref, gain_ref.at[pp], o_ref.at[pp], *scratch)


def _sb_attn(h, gain, batch, layer):
    rows = h.shape[1]
    seq = rows // batch
    n_pairs = D_SB // LANES
    col_blocks = D_SB // LANES
    uu = _sb_cumsum_rhs()
    return pl.pallas_call(
        _sb_kernel,
        grid=(batch, n_pairs // SB_SLABS),
        in_specs=[
            pl.BlockSpec((SB_SLABS, seq, LANES), lambda b, p: (p, b, 0)),
            pl.BlockSpec((SB_SLABS, seq, LANES), lambda b, p: (col_blocks // SB_SLABS + p, b, 0)),
            pl.BlockSpec((SB_SLABS, seq, LANES), lambda b, p: (2 * col_blocks // SB_SLABS + p, b, 0)),
            pl.BlockSpec(uu.shape, lambda b, p: (0, 0)),
            pl.BlockSpec((None, SB_SLABS, 1, LANES), lambda b, p: (layer, p, 0, 0)),
        ],
        out_specs=pl.BlockSpec((SB_SLABS, seq, LANES), lambda b, p: (p, b, 0)),
        out_shape=jax.ShapeDtypeStruct((n_pairs, rows, LANES), _BF16),
        scratch_shapes=[
            pltpu.VMEM((seq + SB_PAD, LANES), _BF16),
            pltpu.VMEM((seq + SB_PAD, LANES), _BF16),
            pltpu.VMEM((HEADS_PER_BLOCK * seq, LANES), _F32),
            pltpu.VMEM((HEADS_PER_BLOCK * seq, LANES), _F32),
            pltpu.SMEM((seq // SB_QBLOCK,), _F32),
        ],
        compiler_params=pltpu.CompilerParams(
            dimension_semantics=("arbitrary", "arbitrary"), vmem_limit_bytes=VMEM_LIMIT_BYTES),
        name="sb_attn",
    )(h, h, h, uu, gain)


def _ca_bias_table(rel_bias):
    n_heads = rel_bias.shape[0]
    row_len = CA_WINDOW + CA_PAIR
    period = row_len + 1
    n_far = CA_PAD - MAX_REL + 1
    far = rel_bias[:, 2 * MAX_REL:]
    ramp = rel_bias[:, 1:2 * MAX_REL][:, ::-1]
    by_offset = jnp.concatenate([
        jnp.broadcast_to(far, (n_heads, n_far)), ramp,
        jnp.broadcast_to(far, (n_heads, period - n_far - ramp.shape[1]))], axis=1)
    table = jnp.tile(by_offset, (1, CA_PAIR))[:, :CA_PAIR * row_len]
    table = table.reshape(n_heads, CA_PAIR, row_len)[:, :, :CA_WINDOW]
    qi = np.arange(CA_PAIR)[:, None]
    kj = np.arange(CA_WINDOW)[None, :]
    visible = np.where(qi < CHUNK, kj < CA_PAD + CHUNK, kj >= CHUNK)
    table = jnp.where(jnp.asarray(visible)[None], table.astype(_F32), MASK_BIAS)
    return table.reshape(-1, CA_WINDOW)


def _ca_kernel(q_ref, k_ref, v_ref, bias_ref, gain_ref, o_ref, kpad_ref, vpad_ref):
    seq = q_ref.shape[0]
    n_pairs = seq // CA_PAIR
    kpad_ref[0:CA_PAD, :] = jnp.zeros((CA_PAD, LANES), _BF16)
    vpad_ref[0:CA_PAD, :] = jnp.zeros((CA_PAD, LANES), _BF16)
    kpad_ref[CA_PAD:, :] = k_ref[...]
    vpad_ref[CA_PAD:, :] = v_ref[...]

    rows2 = 2 * CA_PAIR
    row = lax.broadcasted_iota(jnp.int32, (rows2, LANES), 0)
    lane = lax.broadcasted_iota(jnp.int32, (rows2, LANES), 1)
    own_scale = jnp.where((row < CA_PAIR) == (lane < HEAD_DIM), ATTN_SCALE, 0.0).astype(_BF16)
    head0 = lax.broadcasted_iota(jnp.int32, (CA_PAIR, LANES), 1) < HEAD_DIM
    gain = gain_ref[...] * math.sqrt(HEAD_DIM)

    def pairs(c2s, first_group):
        if first_group:
            r0s = [c2 * CA_PAIR for c2 in c2s]
        else:
            r0s = [pl.multiple_of(c2 * CA_PAIR, CA_PAIR) for c2 in c2s]
        q2s, scs, ps, denoms = [], [], [], []
        for r0 in r0s:
            q = q_ref[pl.ds(r0, CA_PAIR), :]
            q2s.append(jnp.concatenate([q, q], axis=0) * own_scale)
        for r0, q2 in zip(r0s, q2s):
            sc = _dot_nt(q2, kpad_ref[pl.ds(r0, CA_WINDOW), :]) + bias_ref[...]
            if first_group and r0 < CA_PAD:
                col = lax.broadcasted_iota(jnp.int32, (rows2, CA_WINDOW), 1)
                sc = jnp.where(col >= CA_PAD - r0, sc, MASK_BIAS)
            scs.append(sc)
        for sc in scs:
            p = jnp.exp(sc - jnp.max(sc, axis=1, keepdims=True))
            denoms.append(jnp.sum(p, axis=1, keepdims=True))
            ps.append(p.astype(_BF16))
        accs = [_dot(p, vpad_ref[pl.ds(r0, CA_WINDOW), :]) for r0, p in zip(r0s, ps)]
        for r0, acc, denom in zip(r0s, accs, denoms):
            acc = acc / denom
            o = jnp.where(head0, acc[:CA_PAIR], acc[CA_PAIR:])
            o_ref[pl.ds(r0, CA_PAIR), :] = _head_rms_gain(o, head0, gain).astype(_BF16)

    pairs(list(range(CA_GROUP)), True)

    def later_pairs(it, carry):
        pairs([it * CA_GROUP + g for g in range(CA_GROUP)], False)
        return carry

    lax.fori_loop(1, n_pairs // CA_GROUP, later_pairs, 0)


def _ca_attn(h, bias, gain, batch, layer):
    rows = h.shape[1]
    seq = rows // batch
    n_pairs = D_CA // LANES
    first = 3 * D_SB // LANES
    col_blocks = D_CA // LANES
    return pl.pallas_call(
        _ca_kernel,
        grid=(batch, n_pairs),
        in_specs=[
            pl.BlockSpec((None, seq, LANES), lambda b, p: (first + p, b, 0)),
            pl.BlockSpec((None, seq, LANES), lambda b, p: (first + col_blocks + p, b, 0)),
            pl.BlockSpec((None, seq, LANES), lambda b, p: (first + 2 * col_blocks + p, b, 0)),
            pl.BlockSpec((None, None, HEADS_PER_BLOCK * CA_PAIR, CA_WINDOW), lambda b, p: (layer, p, 0, 0)),
            pl.BlockSpec((None, None, 1, LANES), lambda b, p: (layer, p, 0, 0)),
        ],
        out_specs=pl.BlockSpec((None, seq, LANES), lambda b, p: (p, b, 0)),
        out_shape=jax.ShapeDtypeStruct((n_pairs, rows, LANES), _BF16),
        scratch_shapes=[
            pltpu.VMEM((seq + CA_PAD, LANES), _BF16),
            pltpu.VMEM((seq + CA_PAD, LANES), _BF16),
        ],
        compiler_params=pltpu.CompilerParams(
            dimension_semantics=("arbitrary", "arbitrary"), vmem_limit_bytes=VMEM_LIMIT_BYTES),
        name="ca_attn",
    )(h, h, h, bias, gain)


def _half_bf16(w):
    return (FFN_RESIDUAL * w).astype(_BF16)


def kernel(x, ffn1_w_gate, ffn1_w_up, ffn1_w_down, ln1_g, ln1_b, w_in, rel_bias, sb_out_g, ca_out_g, w_out,
           ln2_g, ln2_b, ffn2_w_gate, ffn2_w_up, ffn2_w_down, ln3_g, ln3_b):
    batch, seq, _ = x.shape
    assert CA_GROUP * CA_PAIR >= CA_PAD and seq % (CA_PAIR * CA_GROUP) == 0 and seq % (SB_QBLOCK * SB_GROUP) == 0 and (batch * seq) % (FFN_SUBTILES * ROW_TILE) == 0
    bf16 = lambda w: w.astype(_BF16)
    ln_rows = lambda a: a.reshape(DEPTH, 1, D_MODEL)
    pair_rows = lambda a: a.reshape(DEPTH, -1, 1, LANES)
    ffn1 = (bf16(ffn1_w_gate), bf16(ffn1_w_up), _half_bf16(ffn1_w_down), ln_rows(ln1_g), ln_rows(ln1_b), bf16(w_in))
    mix = (bf16(w_out), ln_rows(ln2_g), ln_rows(ln2_b),
           bf16(ffn2_w_gate), bf16(ffn2_w_up), _half_bf16(ffn2_w_down), ln_rows(ln3_g), ln_rows(ln3_b))
    ca_bias = _ca_bias_table(rel_bias.reshape(-1, rel_bias.shape[-1])).reshape(
        DEPTH, D_CA // LANES, HEADS_PER_BLOCK * CA_PAIR, CA_WINDOW)
    xf = x.reshape(batch * seq, D_MODEL)
    for l in range(DEPTH):
        xf, h = _ffn_ln_proj(xf, *ffn1, layer=l)
        ma = _sb_attn(h, pair_rows(sb_out_g), batch, l)
        mb = _ca_attn(h, ca_bias, pair_rows(ca_out_g), batch, l)
        xf = _mix_ffn_ln(xf, ma, mb, *mix, layer=l)
    return xf.reshape(batch, seq, D_MODEL)
```

```python
import math

import numpy as np
import jax
import jax.numpy as jnp
from jax import lax
from jax.experimental import pallas as pl
from jax.experimental.pallas import tpu as pltpu

D_MODEL = 1024
DEPTH = 4
HEAD_DIM = 64
D_SB = 512
D_CA = 512
CHUNK = 64
N_PREV_CHUNKS = 8
MAX_REL = 128
D_FF = 2816
FFN_RESIDUAL = 0.5
ALPHA = (2 * DEPTH) ** 0.25
LN_EPS = 1e-5
RMS_EPS = 1e-6
ATTN_SCALE = 1.0 / math.sqrt(HEAD_DIM)

LANES = 128
HEADS_PER_BLOCK = LANES // HEAD_DIM
VMEM_LIMIT_BYTES = 56 * 1024 * 1024

ROW_TILE = 512
FFN_SUBTILES = 2
FF_TILE = 256
N_FF_TILES = D_FF // FF_TILE
SB_SLABS = 2
SB_QBLOCK = 64
SB_KBLOCK = LANES
SB_WINDOW = 2 * SB_KBLOCK
SB_PAD = SB_WINDOW - SB_QBLOCK
SB_GROUP = 32
CA_GROUP = 32
CA_PAIR = 2 * CHUNK
CA_PAD = N_PREV_CHUNKS * CHUNK
CA_WINDOW = CA_PAD + CA_PAIR
SB_SKIP_LOG = 88.0
MASK_BIAS = -1e30

_F32 = jnp.float32
_BF16 = jnp.bfloat16


def _layer_norm(y, g, b):
    mu = jnp.mean(y, axis=-1, keepdims=True)
    yc = y - mu
    var = jnp.mean(yc * yc, axis=-1, keepdims=True)
    return yc * lax.rsqrt(var + LN_EPS) * g + b


def _dot(a, b):
    return jnp.dot(a, b, preferred_element_type=_F32)


def _dot_nt(a, b):
    return lax.dot_general(a, b, (((1,), (1,)), ((), ())), preferred_element_type=_F32)


def _split_bf16(a):
    hi = lax.bitcast_convert_type(lax.bitcast_convert_type(a, jnp.uint32) & jnp.uint32(0xFFFF0000), _F32)
    return jnp.concatenate([hi.astype(_BF16), (a - hi).astype(_BF16)], axis=1)


def _softplus(z):
    neg_abs = lax.bitcast_convert_type(lax.bitcast_convert_type(z, jnp.uint32) | jnp.uint32(0x80000000), _F32)
    return jnp.maximum(z, 0.0) + jnp.log(1.0 + jnp.exp(neg_abs))


def _head_rms_gain(o, head0, gain_rt_d):
    ss = o * o
    s0 = jnp.sum(jnp.where(head0, ss, 0.0), axis=1, keepdims=True)
    s1 = jnp.sum(jnp.where(head0, 0.0, ss), axis=1, keepdims=True)
    r = jnp.where(head0, lax.rsqrt(s0 + HEAD_DIM * RMS_EPS), lax.rsqrt(s1 + HEAD_DIM * RMS_EPS))
    return o * r * gain_rt_d


def _swiglu_ln_rows(rows, x, xb_ref, wg_ref, wu_ref, wd_ref, g_ref, b_ref, o_ref):
    xb_ref[rows, :] = x.astype(_BF16)
    for f in range(N_FF_TILES):
        cols = slice(f * FF_TILE, (f + 1) * FF_TILE)
        xb = xb_ref[rows, :]
        gate = _dot(xb, wg_ref[:, cols])
        up = _dot(xb, wu_ref[:, cols])
        hidden = (gate * jax.nn.sigmoid(gate) * up).astype(_BF16)
        part = _dot(hidden, wd_ref[cols, :])
        if f == 0:
            o_ref[rows, :] = part
        else:
            o_ref[rows, :] += part
    o_ref[rows, :] = _layer_norm(ALPHA * x + o_ref[rows, :], g_ref[...], b_ref[...])


def _subtile_rows():
    return [slice(t * ROW_TILE, (t + 1) * ROW_TILE) for t in range(FFN_SUBTILES)]


def _ffn_ln_proj_kernel(x_ref, wg_ref, wu_ref, wd_ref, g_ref, b_ref, win_ref, o_ref, h_ref, xb_ref):
    for rows in _subtile_rows():
        _swiglu_ln_rows(rows, x_ref[rows, :], xb_ref, wg_ref, wu_ref, wd_ref, g_ref, b_ref, o_ref)
    slabs_per_tile = IN_PROJ_TILE // LANES
    for rows in _subtile_rows():
        xb_ref[rows, :] = o_ref[rows, :].astype(_BF16)
        for c in range(IN_PROJ_COLS // IN_PROJ_TILE):
            cols = slice(c * IN_PROJ_TILE, (c + 1) * IN_PROJ_TILE)
            res = _dot(xb_ref[rows, :], win_ref[:, cols]).astype(_BF16)
            for j in range(slabs_per_tile):
                h_ref[c * slabs_per_tile + j, rows, :] = res[:, j * LANES:(j + 1) * LANES]


def _mix_ffn_ln_kernel(x_ref, ma_ref, mb_ref, wo_ref, g2_ref, b2_ref, wg_ref, wu_ref, wd_ref, g3_ref, b3_ref,
                       o_ref, x2_ref, xb_ref):
    for rows in _subtile_rows():
        slabs = [ma_ref[p, rows, :] for p in range(D_SB // LANES)] + [mb_ref[p, rows, :] for p in range(D_CA // LANES)]
        y = ALPHA * x_ref[rows, :] + _dot(jnp.concatenate(slabs, axis=1), wo_ref[...])
        x2_ref[rows, :] = _layer_norm(y, g2_ref[...], b2_ref[...])
    for rows in _subtile_rows():
        _swiglu_ln_rows(rows, x2_ref[rows, :], xb_ref, wg_ref, wu_ref, wd_ref, g3_ref, b3_ref, o_ref)


IN_PROJ_COLS = 3 * (D_SB + D_CA)
IN_PROJ_TILE = 512
FFN_STEP_ROWS = FFN_SUBTILES * ROW_TILE
_RESIDENT = dict(pipeline_mode=pl.Buffered(1))
_ROWS_SPEC = pl.BlockSpec((FFN_STEP_ROWS, D_MODEL), lambda i: (i, 0))


def _layer_spec(shape, layer, resident=True):
    return pl.BlockSpec((None,) + shape, lambda i: (layer, 0, 0), **(_RESIDENT if resident else {}))


def _ffn_weight_specs(layer):
    return [
        _layer_spec((D_MODEL, D_FF), layer),
        _layer_spec((D_MODEL, D_FF), layer),
        _layer_spec((D_FF, D_MODEL), layer),
        _layer_spec((1, D_MODEL), layer, resident=False),
        _layer_spec((1, D_MODEL), layer, resident=False),
    ]

_ROWWISE_PARAMS = pltpu.CompilerParams(dimension_semantics=("arbitrary",), vmem_limit_bytes=VMEM_LIMIT_BYTES)


def _slab_spec(n_cols):
    return pl.BlockSpec((n_cols // LANES, FFN_STEP_ROWS, LANES), lambda i: (0, i, 0))


def _ffn_ln_proj(x, wg, wu, wd, g, b, w_in, layer):
    rows = x.shape[0]
    assert rows % FFN_STEP_ROWS == 0
    return pl.pallas_call(
        _ffn_ln_proj_kernel,
        grid=(rows // FFN_STEP_ROWS,),
        in_specs=[_ROWS_SPEC] + _ffn_weight_specs(layer) + [_layer_spec((D_MODEL, IN_PROJ_COLS), layer)],
        out_specs=[_ROWS_SPEC, _slab_spec(IN_PROJ_COLS)],
        out_shape=[jax.ShapeDtypeStruct((rows, D_MODEL), _F32),
                   jax.ShapeDtypeStruct((IN_PROJ_COLS // LANES, rows, LANES), _BF16)],
        scratch_shapes=[pltpu.VMEM((FFN_STEP_ROWS, D_MODEL), _BF16)],
        compiler_params=_ROWWISE_PARAMS,
        name="ffn_ln_proj",
    )(x, wg, wu, wd, g, b, w_in)


def _mix_ffn_ln(x, ma, mb, wo, g2, b2, wg, wu, wd, g3, b3, layer):
    rows = x.shape[0]
    assert rows % FFN_STEP_ROWS == 0
    return pl.pallas_call(
        _mix_ffn_ln_kernel,
        grid=(rows // FFN_STEP_ROWS,),
        in_specs=[_ROWS_SPEC, _slab_spec(D_SB), _slab_spec(D_CA),
                  _layer_spec((D_SB + D_CA, D_MODEL), layer),
                  _layer_spec((1, D_MODEL), layer, resident=False),
                  _layer_spec((1, D_MODEL), layer, resident=False)] + _ffn_weight_specs(layer),
        out_specs=_ROWS_SPEC,
        out_shape=jax.ShapeDtypeStruct((rows, D_MODEL), _F32),
        scratch_shapes=[pltpu.VMEM((FFN_STEP_ROWS, D_MODEL), _F32), pltpu.VMEM((FFN_STEP_ROWS, D_MODEL), _BF16)],
        compiler_params=_ROWWISE_PARAMS,
        name="mix_ffn_ln",
    )(x, ma, mb, wo, g2, b2, wg, wu, wd, g3, b3)


def _sb_cumsum_rhs():
    j = np.arange(SB_KBLOCK)[:, None]
    s = np.arange(SB_KBLOCK)[None, :]
    u = np.concatenate([(j >= s), np.ones((SB_KBLOCK, LANES), bool)], axis=1).astype(np.float32)
    return jnp.asarray(np.concatenate([u, u], axis=0), dtype=_BF16)


def _sb_pair(q_ref, k_ref, v_ref, uu_ref, gain_ref, o_ref, kpad_ref, vpad_ref, acc_ref, spent_ref, flag_ref):
    seq = q_ref.shape[0]
    n_qblocks = seq // SB_QBLOCK
    kpad_ref[0:SB_PAD, :] = jnp.zeros((SB_PAD, LANES), _BF16)
    vpad_ref[0:SB_PAD, :] = jnp.zeros((SB_PAD, LANES), _BF16)
    kpad_ref[SB_PAD:, :] = k_ref[...]
    vpad_ref[SB_PAD:, :] = v_ref[...]

    rows2 = 2 * SB_QBLOCK
    row = lax.broadcasted_iota(jnp.int32, (rows2, LANES), 0)
    lane = lax.broadcasted_iota(jnp.int32, (rows2, LANES), 1)
    own_scale = jnp.where((row < SB_QBLOCK) == (lane < HEAD_DIM), ATTN_SCALE, 0.0).astype(_BF16)
    visible = lane - (SB_KBLOCK - SB_QBLOCK) < (row & (SB_QBLOCK - 1))
    head0 = lax.broadcasted_iota(jnp.int32, (SB_QBLOCK, LANES), 1) < HEAD_DIM
    gain = gain_ref[...] * math.sqrt(HEAD_DIM)

    def windows(ms):
        r0s = [pl.multiple_of(m * SB_QBLOCK, SB_QBLOCK) for m in ms]
        q2s, zs, sps = [], [], []
        for r0 in r0s:
            q = q_ref[pl.ds(r0, SB_QBLOCK), :]
            q2s.append(jnp.concatenate([q, q], axis=0) * own_scale)
        for r0, q2 in zip(r0s, q2s):
            zs.append(_dot_nt(q2, kpad_ref[pl.ds(r0, SB_WINDOW), :]))
        zs = [(z[:, :SB_KBLOCK], jnp.where(visible, z[:, SB_KBLOCK:], MASK_BIAS)) for z in zs]
        for z_far, z_near in zs:
            sps.append(_softplus(z_far))
            sps.append(_softplus(z_near))
        cs = _dot(_split_bf16(jnp.concatenate(sps, axis=0)), uu_ref[...])
        ws, spents = [], []
        for g, (z_far, z_near) in enumerate(zs):
            far = cs[(2 * g) * rows2:(2 * g + 1) * rows2]
            near = cs[(2 * g + 1) * rows2:(2 * g + 2) * rows2]
            tot_near = near[:, SB_KBLOCK:]
            w_far = jnp.exp(z_far - (far[:, :SB_KBLOCK] + tot_near))
            w_near = jnp.exp(z_near - near[:, :SB_KBLOCK])
            ws.append(jnp.concatenate([w_far, w_near], axis=1).astype(_BF16))
            spents.append(far[:, SB_KBLOCK:] + tot_near)
        accs = [_dot(w, vpad_ref[pl.ds(r0, SB_WINDOW), :]) for r0, w in zip(r0s, ws)]
        return r0s, accs, spents

    def finish(r0, acc):
        o = jnp.where(head0, acc[:SB_QBLOCK], acc[SB_QBLOCK:])
        o_ref[pl.ds(r0, SB_QBLOCK), :] = _head_rms_gain(o, head0, gain).astype(_BF16)

    def first_pass(it, carry):
        ms = [it * SB_GROUP + g for g in range(SB_GROUP)]
        r0s, accs, spents = windows(ms)
        for m, r0, acc, spent in zip(ms, r0s, accs, spents):
            finish(r0, acc)
            a0 = pl.multiple_of(m * rows2, rows2)
            acc_ref[pl.ds(a0, rows2), :] = acc
            spent_ref[pl.ds(a0, rows2), :] = spent
            flag_ref[m] = jnp.min(spent)
        return carry

    lax.fori_loop(0, n_qblocks // SB_GROUP, first_pass, 0)

    def second_pass(m, carry):
        @pl.when(flag_ref[m] <= SB_SKIP_LOG)
        def _():
            r0 = pl.multiple_of(m * SB_QBLOCK, SB_QBLOCK)
            a0 = pl.multiple_of(m * rows2, rows2)
            q = q_ref[pl.ds(r0, SB_QBLOCK), :]
            q2 = jnp.concatenate([q, q], axis=0) * own_scale

            def more(state):
                j, _, spent = state
                return jnp.logical_and(r0 - j * SB_KBLOCK > SB_PAD, jnp.min(spent) <= SB_SKIP_LOG)

            def key_block(state):
                j, acc, spent = state
                rr = pl.multiple_of(r0 - (j + 1) * SB_KBLOCK, SB_QBLOCK)
                kb = kpad_ref[pl.ds(rr, SB_KBLOCK), :]
                vb = vpad_ref[pl.ds(rr, SB_KBLOCK), :]
                z = _dot_nt(q2, kb)
                cs = _dot(_split_bf16(_softplus(z)), uu_ref[...])
                w = jnp.exp(z - (cs[:, :SB_KBLOCK] + spent))
                return j + 1, acc + _dot(w.astype(_BF16), vb), spent + cs[:, SB_KBLOCK:]

            start = (0, acc_ref[pl.ds(a0, rows2), :], spent_ref[pl.ds(a0, rows2), :])
            _, acc, _ = lax.while_loop(more, key_block, start)
            finish(r0, acc)
        return carry

    lax.fori_loop(0, n_qblocks, second_pass, 0)


def _sb_kernel(q_ref, k_ref, v_ref, uu_ref, gain_ref, o_ref, *scratch):
    for pp in range(SB_SLABS):
        _sb_pair(q_ref.at[pp], k_ref.at[pp], v_ref.at[pp], uu_ref, gain_ref.at[pp], o_ref.at[pp], *scratch)


def _sb_attn(h, gain, batch, layer):
    rows = h.shape[1]
    seq = rows // batch
    n_pairs = D_SB // LANES
    col_blocks = D_SB // LANES
    uu = _sb_cumsum_rhs()
    return pl.pallas_call(
        _sb_kernel,
        grid=(batch, n_pairs // SB_SLABS),
        in_specs=[
            pl.BlockSpec((SB_SLABS, seq, LANES), lambda b, p: (p, b, 0)),
            pl.BlockSpec((SB_SLABS, seq, LANES), lambda b, p: (col_blocks // SB_SLABS + p, b, 0)),
            pl.BlockSpec((SB_SLABS, seq, LANES), lambda b, p: (2 * col_blocks // SB_SLABS + p, b, 0)),
            pl.BlockSpec(uu.shape, lambda b, p: (0, 0)),
            pl.BlockSpec((None, SB_SLABS, 1, LANES), lambda b, p: (layer, p, 0, 0)),
        ],
        out_specs=pl.BlockSpec((SB_SLABS, seq, LANES), lambda b, p: (p, b, 0)),
        out_shape=jax.ShapeDtypeStruct((n_pairs, rows, LANES), _BF16),
        scratch_shapes=[
            pltpu.VMEM((seq + SB_PAD, LANES), _BF16),
            pltpu.VMEM((seq + SB_PAD, LANES), _BF16),
            pltpu.VMEM((HEADS_PER_BLOCK * seq, LANES), _F32),
            pltpu.VMEM((HEADS_PER_BLOCK * seq, LANES), _F32),
            pltpu.SMEM((seq // SB_QBLOCK,), _F32),
        ],
        compiler_params=pltpu.CompilerParams(
            dimension_semantics=("arbitrary", "arbitrary"), vmem_limit_bytes=VMEM_LIMIT_BYTES),
        name="sb_attn",
    )(h, h, h, uu, gain)


def _ca_bias_table(rel_bias):
    n_heads = rel_bias.shape[0]
    row_len = CA_WINDOW + CA_PAIR
    period = row_len + 1
    n_far = CA_PAD - MAX_REL + 1
    far = rel_bias[:, 2 * MAX_REL:]
    ramp = rel_bias[:, 1:2 * MAX_REL][:, ::-1]
    by_offset = jnp.concatenate([
        jnp.broadcast_to(far, (n_heads, n_far)), ramp,
        jnp.broadcast_to(far, (n_heads, period - n_far - ramp.shape[1]))], axis=1)
    table = jnp.tile(by_offset, (1, CA_PAIR))[:, :CA_PAIR * row_len]
    table = table.reshape(n_heads, CA_PAIR, row_len)[:, :, :CA_WINDOW]
    qi = np.arange(CA_PAIR)[:, None]
    kj = np.arange(CA_WINDOW)[None, :]
    visible = np.where(qi < CHUNK, kj < CA_PAD + CHUNK, kj >= CHUNK)
    table = jnp.where(jnp.asarray(visible)[None], table.astype(_F32), MASK_BIAS)
    return table.reshape(-1, CA_WINDOW)


def _ca_kernel(q_ref, k_ref, v_ref, bias_ref, gain_ref, o_ref, kpad_ref, vpad_ref):
    seq = q_ref.shape[0]
    n_pairs = seq // CA_PAIR
    kpad_ref[0:CA_PAD, :] = jnp.zeros((CA_PAD, LANES), _BF16)
    vpad_ref[0:CA_PAD, :] = jnp.zeros((CA_PAD, LANES), _BF16)
    kpad_ref[CA_PAD:, :] = k_ref[...]
    vpad_ref[CA_PAD:, :] = v_ref[...]

    rows2 = 2 * CA_PAIR
    row = lax.broadcasted_iota(jnp.int32, (rows2, LANES), 0)
    lane = lax.broadcasted_iota(jnp.int32, (rows2, LANES), 1)
    own_scale = jnp.where((row < CA_PAIR) == (lane < HEAD_DIM), ATTN_SCALE, 0.0).astype(_BF16)
    head0 = lax.broadcasted_iota(jnp.int32, (CA_PAIR, LANES), 1) < HEAD_DIM
    gain = gain_ref[...] * math.sqrt(HEAD_DIM)

    def pairs(c2s, first_group):
        if first_group:
            r0s = [c2 * CA_PAIR for c2 in c2s]
        else:
            r0s = [pl.multiple_of(c2 * CA_PAIR, CA_PAIR) for c2 in c2s]
        q2s, scs, ps, denoms = [], [], [], []
        for r0 in r0s:
            q = q_ref[pl.ds(r0, CA_PAIR), :]
            q2s.append(jnp.concatenate([q, q], axis=0) * own_scale)
        for r0, q2 in zip(r0s, q2s):
            sc = _dot_nt(q2, kpad_ref[pl.ds(r0, CA_WINDOW), :]) + bias_ref[...]
            if first_group and r0 < CA_PAD:
                col = lax.broadcasted_iota(jnp.int32, (rows2, CA_WINDOW), 1)
                sc = jnp.where(col >= CA_PAD - r0, sc, MASK_BIAS)
            scs.append(sc)
        for sc in scs:
            p = jnp.exp(sc - jnp.max(sc, axis=1, keepdims=True))
            denoms.append(jnp.sum(p, axis=1, keepdims=True))
            ps.append(p.astype(_BF16))
        accs = [_dot(p, vpad_ref[pl.ds(r0, CA_WINDOW), :]) for r0, p in zip(r0s, ps)]
        for r0, acc, denom in zip(r0s, accs, denoms):
            acc = acc / denom
            o = jnp.where(head0, acc[:CA_PAIR], acc[CA_PAIR:])
            o_ref[pl.ds(r0, CA_PAIR), :] = _head_rms_gain(o, head0, gain).astype(_BF16)

    pairs(list(range(CA_GROUP)), True)

    def later_pairs(it, carry):
        pairs([it * CA_GROUP + g for g in range(CA_GROUP)], False)
        return carry

    lax.fori_loop(1, n_pairs // CA_GROUP, later_pairs, 0)


def _ca_attn(h, bias, gain, batch, layer):
    rows = h.shape[1]
    seq = rows // batch
    n_pairs = D_CA // LANES
    first = 3 * D_SB // LANES
    col_blocks = D_CA // LANES
    return pl.pallas_call(
        _ca_kernel,
        grid=(batch, n_pairs),
        in_specs=[
            pl.BlockSpec((None, seq, LANES), lambda b, p: (first + p, b, 0)),
            pl.BlockSpec((None, seq, LANES), lambda b, p: (first + col_blocks + p, b, 0)),
            pl.BlockSpec((None, seq, LANES), lambda b, p: (first + 2 * col_blocks + p, b, 0)),
            pl.BlockSpec((None, None, HEADS_PER_BLOCK * CA_PAIR, CA_WINDOW), lambda b, p: (layer, p, 0, 0)),
            pl.BlockSpec((None, None, 1, LANES), lambda b, p: (layer, p, 0, 0)),
        ],
        out_specs=pl.BlockSpec((None, seq, LANES), lambda b, p: (p, b, 0)),
        out_shape=jax.ShapeDtypeStruct((n_pairs, rows, LANES), _BF16),
        scratch_shapes=[
            pltpu.VMEM((seq + CA_PAD, LANES), _BF16),
            pltpu.VMEM((seq + CA_PAD, LANES), _BF16),
        ],
        compiler_params=pltpu.CompilerParams(
            dimension_semantics=("arbitrary", "arbitrary"), vmem_limit_bytes=VMEM_LIMIT_BYTES),
        name="ca_attn",
    )(h, h, h, bias, gain)


def _half_bf16(w):
    return (FFN_RESIDUAL * w).astype(_BF16)


def kernel(x, ffn1_w_gate, ffn1_w_up, ffn1_w_down, ln1_g, ln1_b, w_in, rel_bias, sb_out_g, ca_out_g, w_out,
           ln2_g, ln2_b, ffn2_w_gate, ffn2_w_up, ffn2_w_down, ln3_g, ln3_b):
    batch, seq, _ = x.shape
    assert CA_GROUP * CA_PAIR >= CA_PAD and seq % (CA_PAIR * CA_GROUP) == 0 and seq % (SB_QBLOCK * SB_GROUP) == 0 and (batch * seq) % (FFN_SUBTILES * ROW_TILE) == 0
    bf16 = lambda w: w.astype(_BF16)
    ln_rows = lambda a: a.reshape(DEPTH, 1, D_MODEL)
    pair_rows = lambda a: a.reshape(DEPTH, -1, 1, LANES)
    ffn1 = (bf16(ffn1_w_gate), bf16(ffn1_w_up), _half_bf16(ffn1_w_down), ln_rows(ln1_g), ln_rows(ln1_b), bf16(w_in))
    mix = (bf16(w_out), ln_rows(ln2_g), ln_rows(ln2_b),
           bf16(ffn2_w_gate), bf16(ffn2_w_up), _half_bf16(ffn2_w_down), ln_rows(ln3_g), ln_rows(ln3_b))
    ca_bias = _ca_bias_table(rel_bias.reshape(-1, rel_bias.shape[-1])).reshape(
        DEPTH, D_CA // LANES, HEADS_PER_BLOCK * CA_PAIR, CA_WINDOW)
    xf = x.reshape(batch * seq, D_MODEL)
    for l in range(DEPTH):
        xf, h = _ffn_ln_proj(xf, *ffn1, layer=l)
        ma = _sb_attn(h, pair_rows(sb_out_g), batch, l)
        mb = _ca_attn(h, ca_bias, pair_rows(ca_out_g), batch, l)
        xf = _mix_ffn_ln(xf, ma, mb, *mix, layer=l)
    return xf.reshape(batch, seq, D_MODEL)
```
